```python
import math
import jax, jax.numpy as jnp
from jax import lax
import numpy as np

D_MODEL = 1024
BATCH = 16
SEQ = 4096
DEPTH = 4

N_MIXERS = 2
N_RET_LAYERS = (DEPTH + 1) // 2
N_FNO_LAYERS = DEPTH // 2
RET_HEADS = 4
RET_DK = 256
RET_DV = 512
RET_QK = RET_HEADS * RET_DK
RET_V = RET_HEADS * RET_DV
RET_IN = 2 * RET_QK + 2 * RET_V
RET_CHUNK = 128
ROPE_BASE = 10000.0
FNO_GROUPS = 4
FNO_GROUP_DIM = D_MODEL // FNO_GROUPS
D_FF = ((8 * D_MODEL // 3 + 255) // 256) * 256
PLE_DIM = 256
NORM_EPS = 1e-6

kernel_name = "bidir_retention_fnet_hybrid"


def _rmsnorm(x, gain):
    xf = x.astype(jnp.float32)
    xf = xf * lax.rsqrt(jnp.mean(xf * xf, axis=-1, keepdims=True) + NORM_EPS)
    return (xf * gain.astype(jnp.float32)).astype(x.dtype)


def _rope(x, positions):
    half = x.shape[-1] // 2
    freq = 1.0 / (ROPE_BASE ** jnp.linspace(0.0, 1.0, half, dtype=jnp.float32))
    ang = positions.astype(jnp.float32)[..., None] * freq
    cos = jnp.cos(ang)[:, :, None, :].astype(x.dtype)
    sin = jnp.sin(ang)[:, :, None, :].astype(x.dtype)
    x1, x2 = x[..., :half], x[..., half:]
    return jnp.concatenate([x1 * cos - x2 * sin, x1 * sin + x2 * cos], axis=-1)


def _chunk_retention(q, k, v, log_gamma, strict):
    B, H, S, dk = q.shape
    dv = v.shape[-1]
    C = RET_CHUNK
    N = S // C
    qc = q.astype(jnp.float32).reshape(B, H, N, C, dk)
    kc = k.astype(jnp.float32).reshape(B, H, N, C, dk)
    vc = v.astype(jnp.float32).reshape(B, H, N, C, dv)
    lg = log_gamma[:, None]
    idx = jnp.arange(C, dtype=jnp.float32)
    diff = idx[:, None] - idx[None, :]
    mask = (diff > 0) if strict else (diff >= 0)
    decay = jnp.where(mask[None], jnp.exp(lg[:, :, None] * jnp.maximum(diff, 0.0)[None]), 0.0)
    scores = jnp.einsum('bhnid,bhnjd->bhnij', qc, kc) * decay[None, :, None]
    inner = jnp.einsum('bhnij,bhnje->bhnie', scores, vc)
    xi = jnp.exp(lg * (idx + 1.0))
    zeta = jnp.exp(lg * (C - 1.0 - idx))
    g_chunk = jnp.exp(lg * float(C))

    def step(R, xs):
        q_n, k_n, v_n = xs
        cross = jnp.einsum('bhcd,bhde->bhce', q_n, R) * xi[None, :, :, None]
        R_new = g_chunk[None, :, :, None] * R + jnp.einsum('bhcd,bhce->bhde', k_n, v_n * zeta[None, :, :, None])
        return R_new, cross

    R0 = jnp.zeros((B, H, dk, dv), jnp.float32)
    xs = (jnp.moveaxis(qc, 2, 0), jnp.moveaxis(kc, 2, 0), jnp.moveaxis(vc, 2, 0))
    _, cross = lax.scan(step, R0, xs)
    out = inner + jnp.moveaxis(cross, 0, 2)
    return out.reshape(B, H, S, dv)


def _retention_mixer(h, positions, w_in, w_out, gn_gain, decay_logit):
    B, S, _ = h.shape
    proj = h @ w_in
    q, k, v, g = jnp.split(proj, [RET_QK, 2 * RET_QK, 2 * RET_QK + RET_V], axis=-1)
    q = _rope(q.reshape(B, S, RET_HEADS, RET_DK), positions)
    k = _rope(k.reshape(B, S, RET_HEADS, RET_DK), positions) * (RET_DK ** -0.5)
    v = v.reshape(B, S, RET_HEADS, RET_DV)
    q, k, v = (t.transpose(0, 2, 1, 3) for t in (q, k, v))
    log_gamma = jax.nn.log_sigmoid(decay_logit.astype(jnp.float32))
    o_fwd = _chunk_retention(q, k, v, log_gamma[0], strict=False)
    o_bwd = jnp.flip(_chunk_retention(jnp.flip(q, 2), jnp.flip(k, 2), jnp.flip(v, 2), log_gamma[1], strict=True), axis=2)
    o = o_fwd + o_bwd
    mu = jnp.mean(o, axis=-1, keepdims=True)
    var = jnp.mean(jnp.square(o - mu), axis=-1, keepdims=True)
    o = (o - mu) * lax.rsqrt(var + NORM_EPS)
    o = o.transpose(0, 2, 1, 3).reshape(B, S, RET_V) * gn_gain.astype(jnp.float32)
    return (jax.nn.silu(g) * o.astype(h.dtype)) @ w_out


def _fourier_mixer(h, w_out):
    B, S, D = h.shape
    hg = h.astype(jnp.float32).reshape(B, S, FNO_GROUPS, FNO_GROUP_DIM)
    f = jnp.real(jnp.fft.fft2(hg, axes=(1, 3), norm='ortho'))
    return f.reshape(B, S, D).astype(h.dtype) @ w_out


def _swiglu(h, w_gate, w_up, w_down):
    return (jax.nn.silu(h @ w_gate) * (h @ w_up)) @ w_down


def setup_inputs(seed: int = 0) -> dict:
    key = jax.random.key(seed)
    ks = jax.random.split(key, 20)
    f32 = jnp.float32
    nrm = lambda k, shape, fan_in: jax.random.normal(k, shape, f32) * (fan_in ** -0.5)
    x = jax.random.normal(ks[0], (BATCH, SEQ, D_MODEL), f32)
    p = jax.random.normal(ks[1], (DEPTH, BATCH, SEQ, PLE_DIM), f32)
    positions = jnp.broadcast_to(jnp.arange(SEQ, dtype=jnp.int32)[None, :], (BATCH, SEQ))
    base_logit = (5.0 + jnp.arange(RET_HEADS, dtype=f32)) * math.log(2.0)
    ret_decay_logit = base_logit[None, None, :] + 0.1 * jax.random.normal(ks[2], (N_RET_LAYERS, 2, RET_HEADS), f32)
    return {
        "x": x,
        "p": p,
        "positions": positions,
        "norm_mix": 1.0 + 0.05 * jax.random.normal(ks[3], (DEPTH, D_MODEL), f32),
        "ret_w_in": nrm(ks[4], (N_RET_LAYERS, D_MODEL, RET_IN), D_MODEL),
        "ret_w_out": nrm(ks[5], (N_RET_LAYERS, RET_V, D_MODEL), RET_V),
        "ret_gn_gain": 1.0 + 0.05 * jax.random.normal(ks[6], (N_RET_LAYERS, RET_V), f32),
        "ret_decay_logit": ret_decay_logit,
        "fno_w_out": nrm(ks[7], (N_FNO_LAYERS, D_MODEL, D_MODEL), D_MODEL),
        "norm_ffn": 1.0 + 0.05 * jax.random.normal(ks[8], (DEPTH, D_MODEL), f32),
        "ffn_w_gate": nrm(ks[9], (DEPTH, D_MODEL, D_FF), D_MODEL),
        "ffn_w_up": nrm(ks[10], (DEPTH, D_MODEL, D_FF), D_MODEL),
        "ffn_w_down": nrm(ks[11], (DEPTH, D_FF, D_MODEL), D_FF),
        "norm_ple": 1.0 + 0.05 * jax.random.normal(ks[12], (DEPTH, D_MODEL), f32),
        "ple_w_gate": nrm(ks[13], (DEPTH, D_MODEL, D_MODEL), D_MODEL),
        "ple_w_proj": nrm(ks[14], (DEPTH, PLE_DIM, D_MODEL), PLE_DIM),
        "final_norm": 1.0 + 0.05 * jax.random.normal(ks[15], (D_MODEL,), f32),
    }


def reference(x, p, positions, norm_mix, ret_w_in, ret_w_out, ret_gn_gain, ret_decay_logit,
              fno_w_out, norm_ffn, ffn_w_gate, ffn_w_up, ffn_w_down,
              norm_ple, ple_w_gate, ple_w_proj, final_norm):
    for i in range(DEPTH):
        h = _rmsnorm(x, norm_mix[i])
        j = i // N_MIXERS
        if i % N_MIXERS == 0:
            x = x + _retention_mixer(h, positions, ret_w_in[j], ret_w_out[j], ret_gn_gain[j], ret_decay_logit[j])
        else:
            x = x + _fourier_mixer(h, fno_w_out[j])
        x = x + _swiglu(_rmsnorm(x, norm_ffn[i]), ffn_w_gate[i], ffn_w_up[i], ffn_w_down[i])
        gate = jax.nn.sigmoid(_rmsnorm(x, norm_ple[i]) @ ple_w_gate[i])
        x = x + gate * (p[i] @ ple_w_proj[i])
    return _rmsnorm(x, final_norm)
```

```python
import functools
import math

import jax
import jax.numpy as jnp
from jax import lax
from jax.experimental import pallas as pl
from jax.experimental.pallas import tpu as pltpu

F32 = jnp.float32
BF16 = jnp.bfloat16

NORM_EPS = 1e-6
ROPE_BASE = 10000.0
RET_HEADS = 4
RET_DK = 256
RET_DV = 512
FNO_GROUP_DIM = 256

VMEM_LIMIT_BYTES = 56 * 1024 * 1024

RET_IN_TOKENS = 512
MID_TOKENS = 256
RET_CHUNK = 256
FOURIER_ROWS = 256


def _resident(shape):
    nd = len(shape)
    return pl.BlockSpec(shape, lambda *_: (0,) * nd, pipeline_mode=pl.Buffered(1))


def _rmsnorm(x, gain):
    ms = jnp.mean(x * x, axis=-1, keepdims=True)
    return x * lax.rsqrt(ms + NORM_EPS) * gain


def _dot(a, b):
    return jnp.dot(a, b, preferred_element_type=F32)


def _ret_in_kernel(x_ref, pos_ref, freq_ref, gain_ref, w_ref, q_ref, k_ref, v_ref, g_ref):
    qk = RET_HEADS * RET_DK
    vw = RET_HEADS * RET_DV
    half = RET_DK // 2
    h = _rmsnorm(x_ref[...], gain_ref[...]).astype(BF16)
    ang = pos_ref[...].astype(F32) * freq_ref[...]
    cos = jnp.cos(ang)
    sin = jnp.sin(ang)

    def rope_store(proj, out_ref, scale):
        for hd in range(RET_HEADS):
            lo = hd * RET_DK
            x1 = proj[:, lo:lo + half]
            x2 = proj[:, lo + half:lo + RET_DK]
            out_ref[:, lo:lo + half] = ((x1 * cos - x2 * sin) * scale).astype(out_ref.dtype)
            out_ref[:, lo + half:lo + RET_DK] = ((x1 * sin + x2 * cos) * scale).astype(out_ref.dtype)

    rope_store(_dot(h, w_ref[:, 0:qk]), q_ref, 1.0)
    rope_store(_dot(h, w_ref[:, qk:2 * qk]), k_ref, RET_DK ** -0.5)
    v_ref[...] = _dot(h, w_ref[:, 2 * qk:2 * qk + vw]).astype(v_ref.dtype)
    g_ref[...] = _dot(h, w_ref[:, 2 * qk + vw:2 * qk + 2 * vw]).astype(g_ref.dtype)


def _ret_in(x, pos, freq, gain, w_in):
    t, d = x.shape
    tm = RET_IN_TOKENS
    qk = RET_HEADS * RET_DK
    vw = RET_HEADS * RET_DV
    tok = lambda width: pl.BlockSpec((tm, width), lambda i: (i, 0))
    return pl.pallas_call(
        _ret_in_kernel,
        grid=(t // tm,),
        in_specs=[tok(d), tok(1), _resident(freq.shape), _resident(gain.shape), _resident(w_in.shape)],
        out_specs=[tok(qk), tok(qk), tok(vw), tok(vw)],
        out_shape=[jax.ShapeDtypeStruct((t, qk), BF16), jax.ShapeDtypeStruct((t, qk), BF16),
                   jax.ShapeDtypeStruct((t, vw), BF16), jax.ShapeDtypeStruct((t, vw), BF16)],
        compiler_params=pltpu.CompilerParams(dimension_semantics=("arbitrary",),
                                             vmem_limit_bytes=VMEM_LIMIT_BYTES),
        name="ret_in",
    )(x, pos, freq, gain, w_in)


def _log_sigmoid(x):
    return jnp.minimum(x, 0.0) - jnp.log(1.0 + jnp.exp(-jnp.abs(x)))


def _ret_core_kernel(logit_ref, q_ref, k_ref, v_ref, g_ref, gain_ref, o_ref, rf_ref, rb_ref, rbs_ref):
    c_len = RET_CHUNK
    seq = q_ref.shape[0]
    n_chunks = seq // c_len
    head = pl.program_id(1)

    def log_gamma(direction, shape):
        return _log_sigmoid(jnp.full(shape, logit_ref[direction, head], F32))

    row = lax.broadcasted_iota(jnp.int32, (c_len, 1), 0).astype(F32)
    lgf_col = log_gamma(0, (c_len, 1))
    lgb_col = log_gamma(1, (c_len, 1))
    xi_f = jnp.exp(lgf_col * (row + 1.0))
    zeta_f = jnp.exp(lgf_col * (c_len - 1.0 - row))
    xi_b = jnp.exp(lgb_col * (c_len - row))
    zeta_b = jnp.exp(lgb_col * row)
    gchunk_f = jnp.exp(log_gamma(0, (1, 1)) * float(c_len))
    gchunk_b = jnp.exp(log_gamma(1, (1, 1)) * float(c_len))

    ii = lax.broadcasted_iota(jnp.int32, (c_len, c_len), 0)
    jj = lax.broadcasted_iota(jnp.int32, (c_len, c_len), 1)
    dist = (ii - jj).astype(F32)
    decay = jnp.where(ii >= jj,
                      jnp.exp(log_gamma(0, (c_len, c_len)) * jnp.maximum(dist, 0.0)),
                      jnp.exp(log_gamma(1, (c_len, c_len)) * jnp.maximum(-dist, 0.0)))

    def chunk(ref, c):
        return ref[pl.ds(pl.multiple_of(c * c_len, c_len), c_len), :]

    def kt_v(k_c, zeta, v_c):
        kz = (k_c.astype(F32) * zeta).astype(BF16)
        return lax.dot_general(kz, v_c, (((0,), (0,)), ((), ())), preferred_element_type=F32)

    rb_ref[...] = jnp.zeros_like(rb_ref)
    rbs_ref[n_chunks - 1] = jnp.zeros(rbs_ref.shape[1:], rbs_ref.dtype)

    def backward_step(i, carry):
        c = n_chunks - 1 - i
        rb = gchunk_b * rb_ref[...] + kt_v(chunk(k_ref, c), zeta_b, chunk(v_ref, c))
        rb_ref[...] = rb
        rbs_ref[c - 1] = rb.astype(rbs_ref.dtype)
        return carry

    lax.fori_loop(0, n_chunks - 1, backward_step, 0)

    rf_ref[...] = jnp.zeros_like(rf_ref)
    gain = gain_ref[...]

    def forward_step(c, update_state):
        q_c, k_c, v_c = chunk(q_ref, c), chunk(k_ref, c), chunk(v_ref, c)
        scores = lax.dot_general(q_c, k_c, (((1,), (1,)), ((), ())), preferred_element_type=F32)
        o = _dot((scores * decay).astype(BF16), v_c)
        o = o + xi_f * _dot(q_c, rf_ref[...].astype(BF16))
        o = o + xi_b * _dot(q_c, rbs_ref[c])
        mu = jnp.mean(o, axis=-1, keepdims=True)
        cen = o - mu
        var = jnp.mean(cen * cen, axis=-1, keepdims=True)
        normed = cen * lax.rsqrt(var + NORM_EPS) * gain
        g_c = chunk(g_ref, c).astype(F32)
        gate = g_c * (1.0 / (1.0 + jnp.exp(-g_c)))
        o_ref[pl.ds(pl.multiple_of(c * c_len, c_len), c_len), :] = (gate * normed).astype(o_ref.dtype)
        if update_state:
            rf_ref[...] = gchunk_f * rf_ref[...] + kt_v(k_c, zeta_f, v_c)

    def forward_body(c, carry):
        forward_step(c, True)
        return carry

    lax.fori_loop(0, n_chunks - 1, forward_body, 0)
    forward_step(n_chunks - 1, False)


def _ret_core(q, k, v, g, gn_gain, decay_logit, batch, seq):
    n_chunks = seq // RET_CHUNK
    blk = lambda width: pl.BlockSpec((seq, width), lambda b, h: (b, h))
    return pl.pallas_call(
        _ret_core_kernel,
        grid=(batch, RET_HEADS),
        in_specs=[pl.BlockSpec(memory_space=pltpu.SMEM),
                  blk(RET_DK), blk(RET_DK), blk(RET_DV), blk(RET_DV),
                  pl.BlockSpec((None, 1, RET_DV), lambda b, h: (h, 0, 0))],
        out_specs=blk(RET_DV),
        out_shape=jax.ShapeDtypeStruct(v.shape, BF16),
        scratch_shapes=[pltpu.VMEM((RET_DK, RET_DV), F32), pltpu.VMEM((RET_DK, RET_DV), F32),
                        pltpu.VMEM((n_chunks, RET_DK, RET_DV), BF16)],
        compiler_params=pltpu.CompilerParams(dimension_semantics=("arbitrary", "arbitrary"),
                                             vmem_limit_bytes=VMEM_LIMIT_BYTES),
        name="ret_core",
    )(decay_logit, q, k, v, g, gn_gain)


def _fourier_kernel(h_ref, cse_ref, cso_ref, wc_ref, ws_ref, f_ref, hs_ref, hd_ref):
    half = h_ref.shape[0] // 2
    d = h_ref.shape[1]
    rows = f_ref.shape[0]

    @pl.when(pl.program_id(1) == 0)
    def _():
        lo = h_ref[0:half, :].astype(F32)
        hi = h_ref[half:2 * half, :].astype(F32)
        hs_ref[...] = (lo + hi).astype(hs_ref.dtype)
        hd_ref[...] = (lo - hi).astype(hd_ref.dtype)

    def parity(cs_ref, src_ref, col0):
        ab = _dot(cs_ref[...], src_ref[...])
        a = ab[0:rows].astype(BF16)
        b = ab[rows:2 * rows].astype(BF16)
        for grp in range(d // FNO_GROUP_DIM):
            lo = grp * FNO_GROUP_DIM
            fg = _dot(a[:, lo:lo + FNO_GROUP_DIM], wc_ref[...]) + _dot(b[:, lo:lo + FNO_GROUP_DIM], ws_ref[...])
            f_ref[:, col0 + lo:col0 + lo + FNO_GROUP_DIM] = fg.astype(f_ref.dtype)

    parity(cse_ref, hs_ref, 0)
    parity(cso_ref, hd_ref, d)


def _dft_tables(seq):
    half = seq // 2
    rows = FOURIER_ROWS
    s = jnp.arange(half, dtype=jnp.int32)[None, :]
    kk = jnp.arange(half, dtype=jnp.int32)[:, None]

    def table(freq):
        phase = ((freq * s) % seq).astype(F32) * (2.0 * math.pi / seq)
        c = jnp.cos(phase).reshape(half // rows, rows, half)
        sn = jnp.sin(phase).reshape(half // rows, rows, half)
        return jnp.concatenate([c, sn], axis=1).astype(BF16)

    return table(2 * kk), table(2 * kk + 1)


def _channel_tables(seq):
    gdim = FNO_GROUP_DIM
    dd = jnp.arange(gdim, dtype=jnp.int32)
    phase = ((dd[:, None] * dd[None, :]) % gdim).astype(F32) * (2.0 * math.pi / gdim)
    scale = 1.0 / math.sqrt(seq * gdim)
    return (jnp.cos(phase) * scale).astype(BF16), (-jnp.sin(phase) * scale).astype(BF16)


def _fourier(h, batch, seq):
    t, d = h.shape
    half = seq // 2
    rows = FOURIER_ROWS
    tiles = half // rows
    cse, cso = _dft_tables(seq)
    wc, ws = _channel_tables(seq)
    table_spec = pl.BlockSpec((None, 2 * rows, half), lambda b, kt: (kt, 0, 0))
    f = pl.pallas_call(
        _fourier_kernel,
        grid=(batch, tiles),
        in_specs=[pl.BlockSpec((seq, d), lambda b, kt: (b, 0)), table_spec, table_spec,
                  _resident(wc.shape), _resident(ws.shape)],
        out_specs=pl.BlockSpec((rows, 2 * d), lambda b, kt: (b * tiles + kt, 0)),
        out_shape=jax.ShapeDtypeStruct((t // 2, 2 * d), BF16),
        scratch_shapes=[pltpu.VMEM((half, d), BF16), pltpu.VMEM((half, d), BF16)],
        compiler_params=pltpu.CompilerParams(dimension_semantics=("arbitrary", "arbitrary"),
                                             vmem_limit_bytes=VMEM_LIMIT_BYTES),
        name="fourier",
    )(h, cse, cso, wc, ws)
    return f.reshape(t, d)


def _mid_kernel(x_ref, m_ref, p_ref, wmix_ref, nffn_ref, wg_ref, wu_ref, wd_ref,
                nple_ref, wpg_ref, wpp_ref, nout_ref, *out_refs, final, emit_h):
    x = x_ref[...] + _dot(m_ref[...], wmix_ref[...])
    h = _rmsnorm(x, nffn_ref[...]).astype(BF16)
    gate = _dot(h, wg_ref[...])
    up = _dot(h, wu_ref[...])
    act = (gate * (1.0 / (1.0 + jnp.exp(-gate))) * up).astype(BF16)
    x = x + _dot(act, wd_ref[...])
    hp = _rmsnorm(x, nple_ref[...]).astype(BF16)
    pgate = 1.0 / (1.0 + jnp.exp(-_dot(hp, wpg_ref[...])))
    x = x + pgate * _dot(p_ref[...].astype(BF16), wpp_ref[...])
    if final:
        out_refs[0][...] = _rmsnorm(x, nout_ref[...])
    else:
        out_refs[0][...] = x
        if emit_h:
            out_refs[1][...] = _rmsnorm(x, nout_ref[...]).astype(BF16)


def _mid(x, m, p, wmix, nffn, wg, wu, wd, nple, wpg, wpp, nout, *, final, emit_h):
    t, d = x.shape
    tm = MID_TOKENS
    tok = lambda width: pl.BlockSpec((tm, width), lambda i: (i, 0))
    out_specs = [tok(d)]
    out_shape = [jax.ShapeDtypeStruct((t, d), F32)]
    if emit_h:
        out_specs.append(tok(d))
        out_shape.append(jax.ShapeDtypeStruct((t, d), BF16))
    weights = (wmix, nffn, wg, wu, wd, nple, wpg, wpp, nout)
    return pl.pallas_call(
        functools.partial(_mid_kernel, final=final, emit_h=emit_h),
        grid=(t // tm,),
        in_specs=[tok(d), tok(m.shape[1]), tok(p.shape[1])] + [_resident(w.shape) for w in weights],
        out_specs=out_specs,
        out_shape=out_shape,
        compiler_params=pltpu.CompilerParams(dimension_semantics=("arbitrary",),
                                             vmem_limit_bytes=VMEM_LIMIT_BYTES),
        name="mid",
    )(x, m, p, *weights)


def kernel(x, p, positions, norm_mix, ret_w_in, ret_w_out, ret_gn_gain, ret_decay_logit, fno_w_out, norm_ffn, ffn_w_gate, ffn_w_up, ffn_w_down, norm_ple, ple_w_gate, ple_w_proj, final_norm):
    batch, seq, d = x.shape
    depth = p.shape[0]
    t = batch * seq
    xt = x.reshape(t, d)
    pos = positions.reshape(t, 1)
    half = RET_DK // 2
    freq = (1.0 / (ROPE_BASE ** jnp.linspace(0.0, 1.0, half, dtype=F32))).reshape(1, half)
    row = lambda v: v.reshape(1, -1)

    h_next = None
    for i in range(depth):
        j = i // 2
        if i % 2 == 0:
            q, k, v, g = _ret_in(xt, pos, freq, row(norm_mix[i]), ret_w_in[j].astype(BF16))
            m = _ret_core(q, k, v, g, ret_gn_gain[j].reshape(RET_HEADS, 1, RET_DV),
                          ret_decay_logit[j].astype(F32), batch, seq)
            wmix = ret_w_out[j]
        else:
            m = _fourier(h_next, batch, seq)
            wmix = fno_w_out[j]
        final = i == depth - 1
        emit_h = (not final) and (i + 1) % 2 == 1
        nout = final_norm if final else norm_mix[i + 1]
        outs = _mid(xt, m, p[i].reshape(t, -1), wmix.astype(BF16), row(norm_ffn[i]),
                    ffn_w_gate[i].astype(BF16), ffn_w_up[i].astype(BF16), ffn_w_down[i].astype(BF16),
                    row(norm_ple[i]), ple_w_gate[i].astype(BF16), ple_w_proj[i].astype(BF16), row(nout),
                    final=final, emit_h=emit_h)
        xt = outs[0]
        h_next = outs[1] if emit_h else None
    return xt.reshape(batch, seq, d)
```

```python
import functools
import math

import jax
import jax.numpy as jnp
from jax import lax
from jax.experimental import pallas as pl
from jax.experimental.pallas import tpu as pltpu

F32 = jnp.float32
BF16 = jnp.bfloat16

NORM_EPS = 1e-6
ROPE_BASE = 10000.0
RET_HEADS = 4
RET_DK = 256
RET_DV = 512
FNO_GROUP_DIM = 256
LANES = 128

VMEM_LIMIT_BYTES = 56 * 1024 * 1024

TOKEN_TILE = 512
SUB_TILE = 256
RET_CHUNK = 256
FOURIER_ROWS = 256


def _resident(shape):
    nd = len(shape)
    return pl.BlockSpec(shape, lambda *_: (0,) * nd, pipeline_mode=pl.Buffered(1))


def _rmsnorm(x, gain):
    ms = jnp.mean(x * x, axis=-1, keepdims=True)
    return x * lax.rsqrt(ms + NORM_EPS) * gain


def _silu(x):
    hx = 0.5 * x
    return hx + hx * jnp.tanh(hx)


def _dot(a, b):
    return jnp.dot(a, b, preferred_element_type=F32)


def _sub_tiles(rows):
    return [pl.ds(r, SUB_TILE) for r in range(0, rows, SUB_TILE)]


def _ret_in_kernel(x_ref, pos_ref, freq_ref, gain_ref, w_ref, q_ref, k_ref, v_ref, g_ref):
    qk = RET_HEADS * RET_DK
    vw = RET_HEADS * RET_DV
    half = RET_DK // 2

    subs = _sub_tiles(x_ref.shape[0])
    hs = [_rmsnorm(x_ref[rows, :], gain_ref[...]).astype(BF16) for rows in subs]

    angs = [pos_ref[rows, :].astype(F32) * freq_ref[...] for rows in subs]
    trig = [(jnp.cos(ang), jnp.sin(ang)) for ang in angs]

    def rope_store(proj, cos_sin, rows, out_ref, scale):
        cos, sin = cos_sin
        for hd in range(RET_HEADS):
            lo = hd * RET_DK
            x1 = proj[:, lo:lo + half]
            x2 = proj[:, lo + half:lo + RET_DK]
            out_ref[rows, lo:lo + half] = ((x1 * cos - x2 * sin) * scale).astype(out_ref.dtype)
            out_ref[rows, lo + half:lo + RET_DK] = ((x1 * sin + x2 * cos) * scale).astype(out_ref.dtype)

    for h, rows in zip(hs, subs):
        v_ref[rows, :] = _dot(h, w_ref[:, 2 * qk:2 * qk + vw]).astype(v_ref.dtype)
    for h, cos_sin, rows in zip(hs, trig, subs):
        rope_store(_dot(h, w_ref[:, 0:qk]), cos_sin, rows, q_ref, 1.0)
    for h, cos_sin, rows in zip(hs, trig, subs):
        rope_store(_dot(h, w_ref[:, qk:2 * qk]), cos_sin, rows, k_ref, RET_DK ** -0.5)
    for h, rows in zip(hs, subs):
        g_ref[rows, :] = _dot(h, w_ref[:, 2 * qk + vw:2 * qk + 2 * vw]).astype(g_ref.dtype)


def _ret_in(x, pos, freq, gain, w_in):
    t, d = x.shape
    tm = TOKEN_TILE
    qk = RET_HEADS * RET_DK
    vw = RET_HEADS * RET_DV
    tok = lambda width: pl.BlockSpec((tm, width), lambda i: (i, 0))
    return pl.pallas_call(
        _ret_in_kernel,
        grid=(t // tm,),
        in_specs=[tok(d), tok(1), _resident(freq.shape), _resident(gain.shape), _resident(w_in.shape)],
        out_specs=[tok(qk), tok(qk), tok(vw), tok(vw)],
        out_shape=[jax.ShapeDtypeStruct((t, qk), BF16), jax.ShapeDtypeStruct((t, qk), BF16),
                   jax.ShapeDtypeStruct((t, vw), BF16), jax.ShapeDtypeStruct((t, vw), BF16)],
        compiler_params=pltpu.CompilerParams(dimension_semantics=("arbitrary",),
                                             vmem_limit_bytes=VMEM_LIMIT_BYTES),
        name="ret_in",
    )(x, pos, freq, gain, w_in)


def _log_sigmoid(x):
    return jnp.minimum(x, 0.0) - jnp.log(1.0 + jnp.exp(-jnp.abs(x)))


def _ret_core_kernel(logit_ref, q_ref, k_ref, v_ref, g_ref, o_ref, rf_ref, rb_ref, rfs_ref, rbs_ref):
    c_len = RET_CHUNK
    n_chunks = q_ref.shape[0] // c_len
    head = pl.program_id(1)

    def log_gamma(direction, shape):
        return _log_sigmoid(jnp.full(shape, logit_ref[direction, head], F32))

    row = lax.broadcasted_iota(jnp.int32, (c_len, RET_DK), 0).astype(F32)
    lgf = log_gamma(0, (c_len, RET_DK))
    lgb = log_gamma(1, (c_len, RET_DK))
    xi_f = jnp.exp(lgf * (row + 1.0)).astype(BF16)
    zeta_f = jnp.exp(lgf * (c_len - 1.0 - row)).astype(BF16)
    xi_b = jnp.exp(lgb * (c_len - row)).astype(BF16)
    zeta_b = jnp.exp(lgb * row).astype(BF16)
    gchunk_f = jnp.exp(log_gamma(0, (1, 1)) * float(c_len))
    gchunk_b = jnp.exp(log_gamma(1, (1, 1)) * float(c_len))

    ii = lax.broadcasted_iota(jnp.int32, (c_len, c_len), 0)
    jj = lax.broadcasted_iota(jnp.int32, (c_len, c_len), 1)
    dist = (ii - jj).astype(F32)
    decay = jnp.where(ii >= jj,
                      jnp.exp(log_gamma(0, (c_len, c_len)) * jnp.maximum(dist, 0.0)),
                      jnp.exp(log_gamma(1, (c_len, c_len)) * jnp.maximum(-dist, 0.0)))

    def chunk(ref, c):
        return ref[pl.ds(pl.multiple_of(c * c_len, c_len), c_len), :]

    def kt_v(c, zeta):
        return lax.dot_general(chunk(k_ref, c) * zeta, chunk(v_ref, c), (((0,), (0,)), ((), ())),
                               preferred_element_type=F32)

    rf_ref[...] = jnp.zeros_like(rf_ref)
    rb_ref[...] = jnp.zeros_like(rb_ref)
    rfs_ref[0] = jnp.zeros(rfs_ref.shape[1:], rfs_ref.dtype)
    rbs_ref[n_chunks - 1] = jnp.zeros(rbs_ref.shape[1:], rbs_ref.dtype)

    def state_step(i, carry):
        cb = n_chunks - 1 - i
        rf = gchunk_f * rf_ref[...] + kt_v(i, zeta_f)
        rf_ref[...] = rf
        rfs_ref[i + 1] = rf.astype(rfs_ref.dtype)
        rb = gchunk_b * rb_ref[...] + kt_v(cb, zeta_b)
        rb_ref[...] = rb
        rbs_ref[cb - 1] = rb.astype(rbs_ref.dtype)
        return carry

    lax.fori_loop(0, n_chunks - 1, state_step, 0, unroll=True)

    def retention(c):
        q_c = chunk(q_ref, c)
        scores = lax.dot_general(q_c, chunk(k_ref, c), (((1,), (1,)), ((), ())), preferred_element_type=F32)
        return (_dot((scores * decay).astype(BF16), chunk(v_ref, c))
                + _dot(q_c * xi_f, rfs_ref[c]) + _dot(q_c * xi_b, rbs_ref[c]))

    def norm_gate_store(c, o):
        mu = jnp.mean(o, axis=-1, keepdims=True)
        cen = o - mu
        var = jnp.mean(cen * cen, axis=-1, keepdims=True)
        normed = (cen * lax.rsqrt(var + NORM_EPS)).astype(BF16)
        o_ref[pl.ds(pl.multiple_of(c * c_len, c_len), c_len), :] = _silu(chunk(g_ref, c)) * normed

    def out_step(c, carry):
        norm_gate_store(c, retention(c))
        return carry

    lax.fori_loop(0, n_chunks, out_step, 0, unroll=True)


def _ret_core(q, k, v, g, decay_logit, batch, seq):
    n_chunks = seq // RET_CHUNK
    blk = lambda width: pl.BlockSpec((seq, width), lambda b, h: (b, h))
    return pl.pallas_call(
        _ret_core_kernel,
        grid=(batch, RET_HEADS),
        in_specs=[pl.BlockSpec(memory_space=pltpu.SMEM), blk(RET_DK), blk(RET_DK), blk(RET_DV), blk(RET_DV)],
        out_specs=blk(RET_DV),
        out_shape=jax.ShapeDtypeStruct(v.shape, BF16),
        scratch_shapes=[pltpu.VMEM((RET_DK, RET_DV), F32), pltpu.VMEM((RET_DK, RET_DV), F32),
                        pltpu.VMEM((n_chunks, RET_DK, RET_DV), BF16),
                        pltpu.VMEM((n_chunks, RET_DK, RET_DV), BF16)],
        compiler_params=pltpu.CompilerParams(dimension_semantics=("arbitrary", "arbitrary"),
                                             vmem_limit_bytes=VMEM_LIMIT_BYTES),
        name="ret_core",
    )(decay_logit, q, k, v, g)


def _fourier_kernel(he_ref, ho_ref, cse_ref, cso_ref, wc_ref, ws_ref, f_ref):
    rows = f_ref.shape[1]
    d = he_ref.shape[1]

    def transform(cs_ref, src_ref):
        ab = _dot(cs_ref[...], src_ref[...])
        a = ab[0:rows].astype(BF16)
        b = ab[rows:2 * rows].astype(BF16)
        return [_dot(a[:, lo:lo + FNO_GROUP_DIM], wc_ref[...]) + _dot(b[:, lo:lo + FNO_GROUP_DIM], ws_ref[...])
                for lo in range(0, d, FNO_GROUP_DIM)]

    p = transform(cse_ref, he_ref)
    q = transform(cso_ref, ho_ref)
    for grp, (pg, qg) in enumerate(zip(p, q)):
        lo = grp * FNO_GROUP_DIM
        f_ref[0, :, lo:lo + FNO_GROUP_DIM] = (pg + qg).astype(f_ref.dtype)
        f_ref[1, :, lo:lo + FNO_GROUP_DIM] = (pg - qg).astype(f_ref.dtype)


def _dft_tables(seq):
    half = seq // 2
    rows = FOURIER_ROWS
    s = jnp.arange(half, dtype=jnp.int32)[None, :]
    k = jnp.arange(half, dtype=jnp.int32)[:, None]

    def table(phase_index):
        phase = (phase_index % seq).astype(F32) * (2.0 * math.pi / seq)
        c = jnp.cos(phase).reshape(half // rows, rows, half)
        sn = jnp.sin(phase).reshape(half // rows, rows, half)
        return jnp.concatenate([c, sn], axis=1).astype(BF16)

    return table(k * (2 * s)), table(k * (2 * s + 1))


def _channel_tables(seq):
    gdim = FNO_GROUP_DIM
    dd = jnp.arange(gdim, dtype=jnp.int32)
    phase = ((dd[:, None] * dd[None, :]) % gdim).astype(F32) * (2.0 * math.pi / gdim)
    scale = 1.0 / math.sqrt(seq * gdim)
    return (jnp.cos(phase) * scale).astype(BF16), (-jnp.sin(phase) * scale).astype(BF16)


def _fourier(he, ho, tables, batch, seq):
    d = he.shape[1]
    half = seq // 2
    rows = FOURIER_ROWS
    cse, cso, wc, ws = tables
    table_spec = pl.BlockSpec((None, 2 * rows, half), lambda b, kt: (kt, 0, 0))
    src_spec = pl.BlockSpec((half, d), lambda b, kt: (b, 0))
    f = pl.pallas_call(
        _fourier_kernel,
        grid=(batch, half // rows),
        in_specs=[src_spec, src_spec, table_spec, table_spec, _resident(wc.shape), _resident(ws.shape)],
        out_specs=pl.BlockSpec((None, 2, rows, d), lambda b, kt: (b, 0, kt, 0)),
        out_shape=jax.ShapeDtypeStruct((batch, 2, half, d), BF16),
        compiler_params=pltpu.CompilerParams(dimension_semantics=("arbitrary", "arbitrary"),
                                             vmem_limit_bytes=VMEM_LIMIT_BYTES),
        name="fourier",
    )(he, ho, cse, cso, wc, ws)
    return f.reshape(batch * seq, d)


def _mid_kernel(x_ref, m_ref, p_ref, mgain_ref, wmix_ref, nffn_ref, wg_ref, wu_ref, wd_ref,
                nple_ref, wpg_ref, wpp_ref, nout_ref, *refs, final, emit_h, scale_m):
    out_ref = refs[0]
    subs = _sub_tiles(x_ref.shape[0])

    def mix_in(rows):
        m = m_ref[rows, :]
        if scale_m:
            m = (m.astype(F32) * mgain_ref[...]).astype(BF16)
        return x_ref[rows, :] + _dot(m, wmix_ref[...])

    def ffn_act(x):
        h = _rmsnorm(x, nffn_ref[...]).astype(BF16)
        return (_silu(_dot(h, wg_ref[...])) * _dot(h, wu_ref[...])).astype(BF16)

    def ple(x, rows):
        hp = _rmsnorm(x, nple_ref[...]).astype(BF16)
        pgate = 0.5 + 0.5 * jnp.tanh(0.5 * _dot(hp, wpg_ref[...]))
        return x + pgate * _dot(p_ref[rows, :].astype(BF16), wpp_ref[...])

    xs = [mix_in(rows) for rows in subs]
    acts = [ffn_act(x) for x in xs]
    xs = [x + _dot(act, wd_ref[...]) for x, act in zip(xs, acts)]
    xs = [ple(x, rows) for x, rows in zip(xs, subs)]
    for x, rows in zip(xs, subs):
        if final:
            out_ref[rows, :] = _rmsnorm(x, nout_ref[...])
        else:
            out_ref[rows, :] = x
            if emit_h:
                hn = _rmsnorm(x, nout_ref[...])
                for blk in range(refs[3].shape[0]):
                    refs[3][blk, rows, :] = hn[:, blk * LANES:(blk + 1) * LANES]
    if emit_h:
        he_ref, ho_ref, hn_ref = refs[1], refs[2], refs[3]
        n_half = he_ref.shape[0]
        for blk in range(hn_ref.shape[0]):
            cols = slice(blk * LANES, (blk + 1) * LANES)
            he_ref[:, cols] = hn_ref[blk, pl.ds(0, n_half, stride=2), :].astype(he_ref.dtype)
            ho_ref[:, cols] = hn_ref[blk, pl.ds(1, n_half, stride=2), :].astype(ho_ref.dtype)


def _mid(x, m, p, layer, mgain, wmix, nffn, wg, wu, wd, nple, wpg, wpp, nout, *, final, emit_h):
    t, d = x.shape
    tm = TOKEN_TILE
    tok = lambda width: pl.BlockSpec((tm, width), lambda i: (i, 0))
    out_specs = [tok(d)]
    out_shape = [jax.ShapeDtypeStruct((t, d), F32)]
    scratch = []
    if emit_h:
        half_spec = pl.BlockSpec((tm // 2, d), lambda i: (i, 0))
        out_specs += [half_spec, half_spec]
        out_shape += [jax.ShapeDtypeStruct((t // 2, d), BF16)] * 2
        scratch = [pltpu.VMEM((d // LANES, tm, LANES), F32)]
    scale_m = mgain is not None
    if mgain is None:
        mgain = jnp.ones((1, m.shape[1]), F32)
    weights = (mgain, wmix, nffn, wg, wu, wd, nple, wpg, wpp, nout)
    p_spec = pl.BlockSpec((None, tm, p.shape[2]), lambda i: (layer, i, 0))
    return pl.pallas_call(
        functools.partial(_mid_kernel, final=final, emit_h=emit_h, scale_m=scale_m),
        grid=(t // tm,),
        in_specs=[tok(d), tok(m.shape[1]), p_spec] + [_resident(w.shape) for w in weights],
        out_specs=out_specs,
        out_shape=out_shape,
        scratch_shapes=scratch,
        compiler_params=pltpu.CompilerParams(dimension_semantics=("arbitrary",),
                                             vmem_limit_bytes=VMEM_LIMIT_BYTES),
        name="mid",
    )(x, m, p, *weights)


def kernel(x, p, positions, norm_mix, ret_w_in, ret_w_out, ret_gn_gain, ret_decay_logit, fno_w_out, norm_ffn, ffn_w_gate, ffn_w_up, ffn_w_down, norm_ple, ple_w_gate, ple_w_proj, final_norm):
    batch, seq, d = x.shape
    depth = p.shape[0]
    t = batch * seq
    xt = x.reshape(t, d)
    pt = p.reshape(depth, t, p.shape[-1])
    pos = positions.reshape(t, 1)
    half = RET_DK // 2
    freq = (1.0 / (ROPE_BASE ** jnp.linspace(0.0, 1.0, half, dtype=F32))).reshape(1, half)
    fourier_tables = _dft_tables(seq) + _channel_tables(seq)
    row = lambda v: v.reshape(1, -1)

    h_next = None
    for i in range(depth):
        j = i // 2
        if i % 2 == 0:
            q, k, v, g = _ret_in(xt, pos, freq, row(norm_mix[i]), ret_w_in[j].astype(BF16))
            m = _ret_core(q, k, v, g, ret_decay_logit[j].astype(F32), batch, seq)
            mgain = row(ret_gn_gain[j])
            wmix = ret_w_out[j]
        else:
            m = _fourier(*h_next, fourier_tables, batch, seq)
            mgain = None
            wmix = fno_w_out[j]
        final = i == depth - 1
        emit_h = (not final) and (i + 1) % 2 == 1
        nout = final_norm if final else norm_mix[i + 1]
        outs = _mid(xt, m, pt, i, mgain, wmix.astype(BF16), row(norm_ffn[i]),
                    ffn_w_gate[i].astype(BF16), ffn_w_up[i].astype(BF16), ffn_w_down[i].astype(BF16),
                    row(norm_ple[i]), ple_w_gate[i].astype(BF16), ple_w_proj[i].astype(BF16), row(nout),
                    final=final, emit_h=emit_h)
        xt = outs[0]
        h_next = (outs[1], outs[2]) if emit_h else None
    return xt.reshape(batch, seq, d)
```

```python
import functools
import math

import jax
import jax.numpy as jnp
from jax import lax
from jax.experimental import pallas as pl
from jax.experimental.pallas import tpu as pltpu

F32 = jnp.float32
BF16 = jnp.bfloat16

NORM_EPS = 1e-6
ROPE_BASE = 10000.0
RET_HEADS = 4
RET_DK = 256
RET_DV = 512
FNO_GROUP_DIM = 256
LANES = 128

VMEM_LIMIT_BYTES = 56 * 1024 * 1024

TOKEN_TILE = 512
SUB_TILE = 256
RET_CHUNK = 256
FOURIER_ROWS = 256


def _resident(shape):
    nd = len(shape)
    return pl.BlockSpec(shape, lambda *_: (0,) * nd, pipeline_mode=pl.Buffered(1))


def _rmsnorm(x, gain):
    ms = jnp.mean(x * x, axis=-1, keepdims=True)
    return x * lax.rsqrt(ms + NORM_EPS) * gain


def _silu(x):
    hx = 0.5 * x
    return hx + hx * jnp.tanh(hx)


def _dot(a, b):
    return jnp.dot(a, b, preferred_element_type=F32)


_TWO_OVER_PI = 0.6366197723675814
_PIO2_HI = 1.5703125
_PIO2_MID = 4.837512969970703125e-4
_PIO2_LO = 7.54978995489188216e-8


def _sincos(x):
    n = jnp.floor(x * _TWO_OVER_PI + 0.5)
    r = ((x - n * _PIO2_HI) - n * _PIO2_MID) - n * _PIO2_LO
    r2 = r * r
    s = r + r * r2 * (-1.6666654611e-1 + r2 * (8.3321608736e-3 + r2 * -1.9515295891e-4))
    c = 1.0 - 0.5 * r2 + r2 * r2 * (4.166664568298827e-2 + r2 * (-1.388731625493765e-3 + r2 * 2.443315711809948e-5))
    q = n.astype(jnp.int32)
    odd = (q & 1) == 1
    sin_x = jnp.where(odd, c, s)
    cos_x = jnp.where(odd, s, c)
    sin_x = jnp.where((q & 2) == 2, -sin_x, sin_x)
    cos_x = jnp.where(((q + 1) & 2) == 2, -cos_x, cos_x)
    return sin_x, cos_x


def _sub_tiles(rows):
    return [pl.ds(r, SUB_TILE) for r in range(0, rows, SUB_TILE)]


def _ret_in_kernel(x_ref, pos_ref, freq_ref, gain_ref, w_ref, q_ref, k_ref, v_ref, g_ref):
    qk = RET_HEADS * RET_DK
    vw = RET_HEADS * RET_DV
    half = RET_DK // 2

    subs = _sub_tiles(x_ref.shape[0])
    hs = [_rmsnorm(x_ref[rows, :], gain_ref[...]).astype(BF16) for rows in subs]

    angs = [pos_ref[rows, :].astype(F32) * freq_ref[...] for rows in subs]
    trig = [_sincos(ang) for ang in angs]

    def rope_store(proj, sin_cos, rows, out_ref, scale):
        sin, cos = sin_cos
        for hd in range(RET_HEADS):
            lo = hd * RET_DK
            x1 = proj[:, lo:lo + half]
            x2 = proj[:, lo + half:lo + RET_DK]
            out_ref[rows, lo:lo + half] = ((x1 * cos - x2 * sin) * scale).astype(out_ref.dtype)
            out_ref[rows, lo + half:lo + RET_DK] = ((x1 * sin + x2 * cos) * scale).astype(out_ref.dtype)

    for h, rows in zip(hs, subs):
        v_ref[rows, :] = _dot(h, w_ref[:, 2 * qk:2 * qk + vw]).astype(v_ref.dtype)
    for h, sin_cos, rows in zip(hs, trig, subs):
        rope_store(_dot(h, w_ref[:, 0:qk]), sin_cos, rows, q_ref, 1.0)
    for h, sin_cos, rows in zip(hs, trig, subs):
        rope_store(_dot(h, w_ref[:, qk:2 * qk]), sin_cos, rows, k_ref, RET_DK ** -0.5)
    for h, rows in zip(hs, subs):
        g_ref[rows, :] = _dot(h, w_ref[:, 2 * qk + vw:2 * qk + 2 * vw]).astype(g_ref.dtype)


def _ret_in(x, pos, freq, gain, w_in):
    t, d = x.shape
    tm = TOKEN_TILE
    qk = RET_HEADS * RET_DK
    vw = RET_HEADS * RET_DV
    tok = lambda width: pl.BlockSpec((tm, width), lambda i: (i, 0))
    return pl.pallas_call(
        _ret_in_kernel,
        grid=(t // tm,),
        in_specs=[tok(d), tok(1), _resident(freq.shape), _resident(gain.shape), _resident(w_in.shape)],
        out_specs=[tok(qk), tok(qk), tok(vw), tok(vw)],
        out_shape=[jax.ShapeDtypeStruct((t, qk), BF16), jax.ShapeDtypeStruct((t, qk), BF16),
                   jax.ShapeDtypeStruct((t, vw), BF16), jax.ShapeDtypeStruct((t, vw), BF16)],
        compiler_params=pltpu.CompilerParams(dimension_semantics=("arbitrary",),
                                             vmem_limit_bytes=VMEM_LIMIT_BYTES),
        name="ret_in",
    )(x, pos, freq, gain, w_in)


def _log_sigmoid(x):
    return jnp.minimum(x, 0.0) - jnp.log(1.0 + jnp.exp(-jnp.abs(x)))


def _ret_core_kernel(logit_ref, q_ref, k_ref, v_ref, g_ref, o_ref, rf_ref, rb_ref, rfs_ref, rbs_ref):
    c_len = RET_CHUNK
    n_chunks = q_ref.shape[0] // c_len
    head = pl.program_id(1)

    def log_gamma(direction, shape):
        return _log_sigmoid(jnp.full(shape, logit_ref[direction, head], F32))

    row = lax.broadcasted_iota(jnp.int32, (c_len, RET_DK), 0).astype(F32)
    lgf = log_gamma(0, (c_len, RET_DK))
    lgb = log_gamma(1, (c_len, RET_DK))
    xi_f = jnp.exp(lgf * (row + 1.0)).astype(BF16)
    zeta_f = jnp.exp(lgf * (c_len - 1.0 - row)).astype(BF16)
    xi_b = jnp.exp(lgb * (c_len - row)).astype(BF16)
    zeta_b = jnp.exp(lgb * row).astype(BF16)
    gchunk_f = jnp.exp(log_gamma(0, (1, 1)) * float(c_len))
    gchunk_b = jnp.exp(log_gamma(1, (1, 1)) * float(c_len))

    ii = lax.broadcasted_iota(jnp.int32, (c_len, c_len), 0)
    jj = lax.broadcasted_iota(jnp.int32, (c_len, c_len), 1)
    dist = (ii - jj).astype(F32)
    decay = jnp.where(ii >= jj,
                      jnp.exp(log_gamma(0, (c_len, c_len)) * jnp.maximum(dist, 0.0)),
                      jnp.exp(log_gamma(1, (c_len, c_len)) * jnp.maximum(-dist, 0.0)))

    def chunk(ref, c):
        return ref[pl.ds(pl.multiple_of(c * c_len, c_len), c_len), :]

    def kt_v(c, zeta):
        return lax.dot_general(chunk(k_ref, c) * zeta, chunk(v_ref, c), (((0,), (0,)), ((), ())),
                               preferred_element_type=F32)

    rf_ref[...] = jnp.zeros_like(rf_ref)
    rb_ref[...] = jnp.zeros_like(rb_ref)
    rfs_ref[0] = jnp.zeros(rfs_ref.shape[1:], rfs_ref.dtype)
    rbs_ref[n_chunks - 1] = jnp.zeros(rbs_ref.shape[1:], rbs_ref.dtype)

    def state_step(i, carry):
        cb = n_chunks - 1 - i
        rf = gchunk_f * rf_ref[...] + kt_v(i, zeta_f)
        rf_ref[...] = rf
        rfs_ref[i + 1] = rf.astype(rfs_ref.dtype)
        rb = gchunk_b * rb_ref[...] + kt_v(cb, zeta_b)
        rb_ref[...] = rb
        rbs_ref[cb - 1] = rb.astype(rbs_ref.dtype)
        return carry

    lax.fori_loop(0, n_chunks - 1, state_step, 0, unroll=True)

    def retention(c):
        q_c = chunk(q_ref, c)
        scores = lax.dot_general(q_c, chunk(k_ref, c), (((1,), (1,)), ((), ())), preferred_element_type=F32)
        return (_dot((scores * decay).astype(BF16), chunk(v_ref, c))
                + _dot(q_c * xi_f, rfs_ref[c]) + _dot(q_c * xi_b, rbs_ref[c]))

    def norm_gate_store(c, o):
        mu = jnp.mean(o, axis=-1, keepdims=True)
        cen = o - mu
        var = jnp.mean(cen * cen, axis=-1, keepdims=True)
        normed = (cen * lax.rsqrt(var + NORM_EPS)).astype(BF16)
        o_ref[pl.ds(pl.multiple_of(c * c_len, c_len), c_len), :] = _silu(chunk(g_ref, c)) * normed

    def out_step(c, carry):
        norm_gate_store(c, retention(c))
        return carry

    lax.fori_loop(0, n_chunks, out_step, 0, unroll=True)


def _ret_core(q, k, v, g, decay_logit, batch, seq):
    n_chunks = seq // RET_CHUNK
    blk = lambda width: pl.BlockSpec((seq, width), lambda b, h: (b, h))
    return pl.pallas_call(
        _ret_core_kernel,
        grid=(batch, RET_HEADS),
        in_specs=[pl.BlockSpec(memory_space=pltpu.SMEM), blk(RET_DK), blk(RET_DK), blk(RET_DV), blk(RET_DV)],
        out_specs=blk(RET_DV),
        out_shape=jax.ShapeDtypeStruct(v.shape, BF16),
        scratch_shapes=[pltpu.VMEM((RET_DK, RET_DV), F32), pltpu.VMEM((RET_DK, RET_DV), F32),
                        pltpu.VMEM((n_chunks, RET_DK, RET_DV), BF16),
                        pltpu.VMEM((n_chunks, RET_DK, RET_DV), BF16)],
        compiler_params=pltpu.CompilerParams(dimension_semantics=("arbitrary", "arbitrary"),
                                             vmem_limit_bytes=VMEM_LIMIT_BYTES),
        name="ret_core",
    )(decay_logit, q, k, v, g)


def _fold_rows(src_ref, sum_ref, dif_ref):
    blk = FOURIER_ROWS
    n = src_ref.shape[0]
    ii = lax.broadcasted_iota(jnp.int32, (blk, blk), 0)
    jj = lax.broadcasted_iota(jnp.int32, (blk, blk), 1)
    flip = jnp.where(ii + jj == blk - 1, 1.0, 0.0).astype(BF16)
    for i in range(n // 2 // blk):
        lo = src_ref[i * blk:(i + 1) * blk, :].astype(F32)
        hi = _dot(flip, src_ref[n - (i + 1) * blk:n - i * blk, :])
        sum_ref[i * blk:(i + 1) * blk, :] = (lo + hi).astype(sum_ref.dtype)
        dif_ref[i * blk:(i + 1) * blk, :] = (lo - hi).astype(dif_ref.dtype)


def _fourier_kernel(ho_ref, heo_ref, hee_ref, qc_ref, qs_ref, pc_ref, ps_ref, pe_ref, wc_ref, ws_ref,
                    f_ref, qsum_ref, qdif_ref, psum_ref, pdif_ref):
    rows = f_ref.shape[1]
    d = ho_ref.shape[1]

    @pl.when(pl.program_id(1) == 0)
    def _():
        _fold_rows(ho_ref, qsum_ref, qdif_ref)
        _fold_rows(heo_ref, psum_ref, pdif_ref)

    def channel_dft(a, b):
        a = a.astype(BF16)
        b = b.astype(BF16)
        return [_dot(a[:, lo:lo + FNO_GROUP_DIM], wc_ref[...]) + _dot(b[:, lo:lo + FNO_GROUP_DIM], ws_ref[...])
                for lo in range(0, d, FNO_GROUP_DIM)]

    qa = _dot(qc_ref[...], qsum_ref[...])
    qb = _dot(qs_ref[...], qdif_ref[...])
    pe = _dot(pe_ref[...], hee_ref[...])
    fq0 = channel_dft(qa[0:rows], qb[0:rows])
    fq1 = channel_dft(qa[rows:2 * rows], qb[rows:2 * rows])
    fpe = channel_dft(pe[0:rows], pe[rows:2 * rows])
    fpo = channel_dft(_dot(pc_ref[...], psum_ref[...]), _dot(ps_ref[...], pdif_ref[...]))
    for grp in range(d // FNO_GROUP_DIM):
        cols = slice(grp * FNO_GROUP_DIM, (grp + 1) * FNO_GROUP_DIM)
        even = fpe[grp] + fpo[grp]
        odd = fpe[grp] - fpo[grp]
        f_ref[0, :, cols] = (even + fq0[grp]).astype(f_ref.dtype)
        f_ref[1, :, cols] = (odd + fq1[grp]).astype(f_ref.dtype)
        f_ref[2, :, cols] = (even - fq0[grp]).astype(f_ref.dtype)
        f_ref[3, :, cols] = (odd - fq1[grp]).astype(f_ref.dtype)


def _fourier_tables(seq):
    n = seq // 4
    rows = FOURIER_ROWS
    tiles = n // rows

    def cis(phase_index):
        phase = (phase_index % seq).astype(F32) * (2.0 * math.pi / seq)
        return jnp.cos(phase), jnp.sin(phase)

    def tiled(x):
        return x.reshape(tiles, rows, x.shape[-1])

    k = jnp.arange(n, dtype=jnp.int32)[:, None]
    j = jnp.arange(n, dtype=jnp.int32)[None, :]
    q_lo = cis(k * (2 * j + 1))
    q_hi = cis((k + n) * (2 * j + 1))
    qc = jnp.concatenate([tiled(q_lo[0]), tiled(q_hi[0])], axis=1).astype(BF16)
    qs = jnp.concatenate([tiled(q_lo[1]), tiled(q_hi[1])], axis=1).astype(BF16)
    po = cis(2 * k * (2 * j[:, :n // 2] + 1))
    pc, ps = tiled(po[0]).astype(BF16), tiled(po[1]).astype(BF16)
    pe = cis(4 * k * j)
    pe = jnp.concatenate([tiled(pe[0]), tiled(pe[1])], axis=1).astype(BF16)

    gdim = FNO_GROUP_DIM
    dd = jnp.arange(gdim, dtype=jnp.int32)
    phase = ((dd[:, None] * dd[None, :]) % gdim).astype(F32) * (2.0 * math.pi / gdim)
    scale = 1.0 / math.sqrt(seq * gdim)
    wc = (jnp.cos(phase) * scale).astype(BF16)
    ws = (-jnp.sin(phase) * scale).astype(BF16)
    return qc, qs, pc, ps, pe, wc, ws


def _fourier(ho, heo, hee, tables, batch, seq):
    d = ho.shape[1]
    n = seq // 4
    rows = FOURIER_ROWS
    qc, qs, pc, ps, pe, wc, ws = tables
    src = lambda nrows: pl.BlockSpec((nrows, d), lambda b, kt: (b, 0))
    tab = lambda t: pl.BlockSpec((None,) + t.shape[1:], lambda b, kt: (kt, 0, 0))
    f = pl.pallas_call(
        _fourier_kernel,
        grid=(batch, n // rows),
        in_specs=[src(2 * n), src(n), src(n), tab(qc), tab(qs), tab(pc), tab(ps), tab(pe),
                  _resident(wc.shape), _resident(ws.shape)],
        out_specs=pl.BlockSpec((None, 4, rows, d), lambda b, kt: (b, 0, kt, 0)),
        out_shape=jax.ShapeDtypeStruct((batch, 4, n, d), BF16),
        scratch_shapes=[pltpu.VMEM((n, d), BF16), pltpu.VMEM((n, d), BF16),
                        pltpu.VMEM((n // 2, d), BF16), pltpu.VMEM((n // 2, d), BF16)],
        compiler_params=pltpu.CompilerParams(dimension_semantics=("arbitrary", "arbitrary"),
                                             vmem_limit_bytes=VMEM_LIMIT_BYTES),
        name="fourier",
    )(ho, heo, hee, qc, qs, pc, ps, pe, wc, ws)
    return f.reshape(batch * seq, d)


def _mid_kernel(x_ref, m_ref, p_ref, mgain_ref, wmix_ref, nffn_ref, wg_ref, wu_ref, wd_ref,
                nple_ref, wpg_ref, wpp_ref, nout_ref, *refs, final, emit_h, scale_m):
    out_ref = refs[0]
    subs = _sub_tiles(x_ref.shape[0])

    def mix_in(rows):
        m = m_ref[rows, :]
        if scale_m:
            m = (m.astype(F32) * mgain_ref[...]).astype(BF16)
        return x_ref[rows, :] + _dot(m, wmix_ref[...])

    def ffn_act(x):
        h = _rmsnorm(x, nffn_ref[...]).astype(BF16)
        return (_silu(_dot(h, wg_ref[...])) * _dot(h, wu_ref[...])).astype(BF16)

    def ple(x, rows):
        hp = _rmsnorm(x, nple_ref[...]).astype(BF16)
        pgate = 0.5 + 0.5 * jnp.tanh(0.5 * _dot(hp, wpg_ref[...]))
        return x + pgate * _dot(p_ref[rows, :].astype(BF16), wpp_ref[...])

    xs = [mix_in(rows) for rows in subs]
    acts = [ffn_act(x) for x in xs]
    xs = [x + _dot(act, wd_ref[...]) for x, act in zip(xs, acts)]
    xs = [ple(x, rows) for x, rows in zip(xs, subs)]
    for x, rows in zip(xs, subs):
        if final:
            out_ref[rows, :] = _rmsnorm(x, nout_ref[...])
        else:
            out_ref[rows, :] = x
            if emit_h:
                hn = _rmsnorm(x, nout_ref[...])
                for blk in range(refs[4].shape[0]):
                    refs[4][blk, rows, :] = hn[:, blk * LANES:(blk + 1) * LANES]
    if emit_h:
        ho_ref, heo_ref, hee_ref, hn_ref = refs[1], refs[2], refs[3], refs[4]
        n = hn_ref.shape[1]
        for blk in range(hn_ref.shape[0]):
            cols = slice(blk * LANES, (blk + 1) * LANES)
            ho_ref[:, cols] = hn_ref[blk, pl.ds(1, n // 2, stride=2), :].astype(ho_ref.dtype)
            heo_ref[:, cols] = hn_ref[blk, pl.ds(2, n // 4, stride=4), :].astype(heo_ref.dtype)
            hee_ref[:, cols] = hn_ref[blk, pl.ds(0, n // 4, stride=4), :].astype(hee_ref.dtype)


def _mid(x, m, p, layer, mgain, wmix, nffn, wg, wu, wd, nple, wpg, wpp, nout, *, final, emit_h):
    t, d = x.shape
    tm = TOKEN_TILE
    tok = lambda width: pl.BlockSpec((tm, width), lambda i: (i, 0))
    out_specs = [tok(d)]
    out_shape = [jax.ShapeDtypeStruct((t, d), F32)]
    scratch = []
    if emit_h:
        for part in (2, 4, 4):
            out_specs.append(pl.BlockSpec((tm // part, d), lambda i: (i, 0)))
            out_shape.append(jax.ShapeDtypeStruct((t // part, d), BF16))
        scratch = [pltpu.VMEM((d // LANES, tm, LANES), F32)]
    scale_m = mgain is not None
    if mgain is None:
        mgain = jnp.ones((1, m.shape[1]), F32)
    weights = (mgain, wmix, nffn, wg, wu, wd, nple, wpg, wpp, nout)
    p_spec = pl.BlockSpec((None, tm, p.shape[2]), lambda i: (layer, i, 0))
    return pl.pallas_call(
        functools.partial(_mid_kernel, final=final, emit_h=emit_h, scale_m=scale_m),
        grid=(t // tm,),
        in_specs=[tok(d), tok(m.shape[1]), p_spec] + [_resident(w.shape) for w in weights],
        out_specs=out_specs,
        out_shape=out_shape,
        scratch_shapes=scratch,
        compiler_params=pltpu.CompilerParams(dimension_semantics=("arbitrary",),
                                             vmem_limit_bytes=VMEM_LIMIT_BYTES),
        name="mid",
    )(x, m, p, *weights)


def kernel(x, p, positions, norm_mix, ret_w_in, ret_w_out, ret_gn_gain, ret_decay_logit, fno_w_out, norm_ffn, ffn_w_gate, ffn_w_up, ffn_w_down, norm_ple, ple_w_gate, ple_w_proj, final_norm):
    batch, seq, d = x.shape
    depth = p.shape[0]
    t = batch * seq
    xt = x.reshape(t, d)
    pt = p.reshape(depth, t, p.shape[-1])
    pos = positions.reshape(t, 1)
    half = RET_DK // 2
    freq = (1.0 / (ROPE_BASE ** jnp.linspace(0.0, 1.0, half, dtype=F32))).reshape(1, half)
    fourier_tables = _fourier_tables(seq)
    row = lambda v: v.reshape(1, -1)

    h_next = None
    for i in range(depth):
        j = i // 2
        if i % 2 == 0:
            q, k, v, g = _ret_in(xt, pos, freq, row(norm_mix[i]), ret_w_in[j].astype(BF16))
            m = _ret_core(q, k, v, g, ret_decay_logit[j].astype(F32), batch, seq)
            mgain = row(ret_gn_gain[j])
            wmix = ret_w_out[j]
        else:
            m = _fourier(*h_next, fourier_tables, batch, seq)
            mgain = None
            wmix = fno_w_out[j]
        final = i == depth - 1
        emit_h = (not final) and (i + 1) % 2 == 1
        nout = final_norm if final else norm_mix[i + 1]
        outs = _mid(xt, m, pt, i, mgain, wmix.astype(BF16), row(norm_ffn[i]),
                    ffn_w_gate[i].astype(BF16), ffn_w_up[i].astype(BF16), ffn_w_down[i].astype(BF16),
                    row(norm_ple[i]), ple_w_gate[i].astype(BF16), ple_w_proj[i].astype(BF16), row(nout),
                    final=final, emit_h=emit_h)
        xt = outs[0]
        h_next = tuple(outs[1:4]) if emit_h else None
    return xt.reshape(batch, seq, d)
```

```python
import functools
import math

import jax
import jax.numpy as jnp
from jax import lax
from jax.experimental import pallas as pl
from jax.experimental.pallas import tpu as pltpu

F32 = jnp.float32
BF16 = jnp.bfloat16

NORM_EPS = 1e-6
ROPE_BASE = 10000.0
RET_HEADS = 4
RET_DK = 256
RET_DV = 512
FNO_GROUP_DIM = 256
LANES = 128

VMEM_LIMIT_BYTES = 56 * 1024 * 1024

TOKEN_TILE = 512
RET_IN_TILE = 1024
SUB_TILE = 256
RET_CHUNK = 256
FOURIER_ROWS = 256


def _resident(shape):
    nd = len(shape)
    return pl.BlockSpec(shape, lambda *_: (0,) * nd, pipeline_mode=pl.Buffered(1))


def _rmsnorm(x, gain):
    ms = jnp.mean(x * x, axis=-1, keepdims=True)
    return x * lax.rsqrt(ms + NORM_EPS) * gain


def _silu(x):
    hx = 0.5 * x
    return hx + hx * jnp.tanh(hx)


def _dot(a, b):
    return jnp.dot(a, b, preferred_element_type=F32)


_TWO_OVER_PI = 0.6366197723675814
_PIO2_HI = 1.5703125
_PIO2_MID = 4.837512969970703125e-4
_PIO2_LO = 7.54978995489188216e-8


def _sincos(x):
    n = jnp.floor(x * _TWO_OVER_PI + 0.5)
    r = ((x - n * _PIO2_HI) - n * _PIO2_MID) - n * _PIO2_LO
    r2 = r * r
    s = r + r * r2 * (-1.6666654611e-1 + r2 * (8.3321608736e-3 + r2 * -1.9515295891e-4))
    c = 1.0 - 0.5 * r2 + r2 * r2 * (4.166664568298827e-2 + r2 * (-1.388731625493765e-3 + r2 * 2.443315711809948e-5))
    q = n.astype(jnp.int32)
    odd = (q & 1) == 1
    sin_x = jnp.where(odd, c, s)
    cos_x = jnp.where(odd, s, c)
    sin_x = jnp.where((q & 2) == 2, -sin_x, sin_x)
    cos_x = jnp.where(((q + 1) & 2) == 2, -cos_x, cos_x)
    return sin_x, cos_x


def _sub_tiles(rows):
    return [pl.ds(r, SUB_TILE) for r in range(0, rows, SUB_TILE)]


def _ret_in_kernel(x_ref, pos_ref, freq_ref, gain_ref, w_ref, q_ref, k_ref, v_ref, g_ref):
    qk = RET_HEADS * RET_DK
    vw = RET_HEADS * RET_DV
    half = RET_DK // 2

    subs = _sub_tiles(x_ref.shape[0])
    hs = [_rmsnorm(x_ref[rows, :], gain_ref[...]).astype(BF16) for rows in subs]

    angs = [pos_ref[rows, :].astype(F32) * freq_ref[...] for rows in subs]
    trig = [_sincos(ang) for ang in angs]

    def rope_store(proj, sin_cos, rows, out_ref, scale):
        sin, cos = sin_cos
        for hd in range(RET_HEADS):
            lo = hd * RET_DK
            x1 = proj[:, lo:lo + half]
            x2 = proj[:, lo + half:lo + RET_DK]
            out_ref[rows, lo:lo + half] = ((x1 * cos - x2 * sin) * scale).astype(out_ref.dtype)
            out_ref[rows, lo + half:lo + RET_DK] = ((x1 * sin + x2 * cos) * scale).astype(out_ref.dtype)

    for h, rows in zip(hs, subs):
        v_ref[rows, :] = _dot(h, w_ref[:, 2 * qk:2 * qk + vw]).astype(v_ref.dtype)
    for h, sin_cos, rows in zip(hs, trig, subs):
        rope_store(_dot(h, w_ref[:, 0:qk]), sin_cos, rows, q_ref, 1.0)
    for h, sin_cos, rows in zip(hs, trig, subs):
        rope_store(_dot(h, w_ref[:, qk:2 * qk]), sin_cos, rows, k_ref, RET_DK ** -0.5)
    for h, rows in zip(hs, subs):
        g_ref[rows, :] = _dot(h, w_ref[:, 2 * qk + vw:2 * qk + 2 * vw]).astype(g_ref.dtype)


def _ret_in(x, pos, freq, gain, w_in):
    t, d = x.shape
    tm = RET_IN_TILE
    qk = RET_HEADS * RET_DK
    vw = RET_HEADS * RET_DV
    tok = lambda width: pl.BlockSpec((tm, width), lambda i: (i, 0))
    return pl.pallas_call(
        _ret_in_kernel,
        grid=(t // tm,),
        in_specs=[tok(d), tok(1), _resident(freq.shape), _resident(gain.shape), _resident(w_in.shape)],
        out_specs=[tok(qk), tok(qk), tok(vw), tok(vw)],
        out_shape=[jax.ShapeDtypeStruct((t, qk), BF16), jax.ShapeDtypeStruct((t, qk), BF16),
                   jax.ShapeDtypeStruct((t, vw), BF16), jax.ShapeDtypeStruct((t, vw), BF16)],
        compiler_params=pltpu.CompilerParams(dimension_semantics=("arbitrary",),
                                             vmem_limit_bytes=VMEM_LIMIT_BYTES),
        name="ret_in",
    )(x, pos, freq, gain, w_in)


def _log_sigmoid(x):
    return jnp.minimum(x, 0.0) - jnp.log(1.0 + jnp.exp(-jnp.abs(x)))


def _ret_core_kernel(logit_ref, q_ref, k_ref, v_ref, g_ref, o_ref, rf_ref, rb_ref, rfs_ref, rbs_ref):
    c_len = RET_CHUNK
    n_chunks = q_ref.shape[0] // c_len
    head = pl.program_id(1)

    def log_gamma(direction, shape):
        return _log_sigmoid(jnp.full(shape, logit_ref[direction, head], F32))

    row = lax.broadcasted_iota(jnp.int32, (c_len, RET_DK), 0).astype(F32)
    lgf = log_gamma(0, (c_len, RET_DK))
    lgb = log_gamma(1, (c_len, RET_DK))
    xi_f = jnp.exp(lgf * (row + 1.0)).astype(BF16)
    zeta_f = jnp.exp(lgf * (c_len - 1.0 - row)).astype(BF16)
    xi_b = jnp.exp(lgb * (c_len - row)).astype(BF16)
    zeta_b = jnp.exp(lgb * row).astype(BF16)
    gchunk_f = jnp.exp(log_gamma(0, (1, 1)) * float(c_len))
    gchunk_b = jnp.exp(log_gamma(1, (1, 1)) * float(c_len))

    ii = lax.broadcasted_iota(jnp.int32, (c_len, c_len), 0)
    jj = lax.broadcasted_iota(jnp.int32, (c_len, c_len), 1)
    dist = (ii - jj).astype(F32)
    decay = jnp.where(ii >= jj,
                      jnp.exp(log_gamma(0, (c_len, c_len)) * jnp.maximum(dist, 0.0)),
                      jnp.exp(log_gamma(1, (c_len, c_len)) * jnp.maximum(-dist, 0.0)))

    def chunk(ref, c):
        return ref[pl.ds(pl.multiple_of(c * c_len, c_len), c_len), :]

    def kt_v(c, zeta):
        return lax.dot_general(chunk(k_ref, c) * zeta, chunk(v_ref, c), (((0,), (0,)), ((), ())),
                               preferred_element_type=F32)

    rf_ref[...] = jnp.zeros_like(rf_ref)
    rb_ref[...] = jnp.zeros_like(rb_ref)
    rfs_ref[0] = jnp.zeros(rfs_ref.shape[1:], rfs_ref.dtype)
    rbs_ref[n_chunks - 1] = jnp.zeros(rbs_ref.shape[1:], rbs_ref.dtype)

    def state_step(i, carry):
        cb = n_chunks - 1 - i
        rf = gchunk_f * rf_ref[...] + kt_v(i, zeta_f)
        rf_ref[...] = rf
        rfs_ref[i + 1] = rf.astype(rfs_ref.dtype)
        rb = gchunk_b * rb_ref[...] + kt_v(cb, zeta_b)
        rb_ref[...] = rb
        rbs_ref[cb - 1] = rb.astype(rbs_ref.dtype)
        return carry

    lax.fori_loop(0, n_chunks - 1, state_step, 0, unroll=True)

    def retention(c):
        q_c = chunk(q_ref, c)
        scores = lax.dot_general(q_c, chunk(k_ref, c), (((1,), (1,)), ((), ())), preferred_element_type=F32)
        return (_dot((scores * decay).astype(BF16), chunk(v_ref, c))
                + _dot(q_c * xi_f, rfs_ref[c]) + _dot(q_c * xi_b, rbs_ref[c]))

    def norm_gate_store(c, o):
        mu = jnp.mean(o, axis=-1, keepdims=True)
        cen = o - mu
        var = jnp.mean(cen * cen, axis=-1, keepdims=True)
        normed = (cen * lax.rsqrt(var + NORM_EPS)).astype(BF16)
        o_ref[pl.ds(pl.multiple_of(c * c_len, c_len), c_len), :] = _silu(chunk(g_ref, c)) * normed

    def out_step(c, carry):
        norm_gate_store(c, retention(c))
        return carry

    lax.fori_loop(0, n_chunks, out_step, 0, unroll=True)


def _ret_core(q, k, v, g, decay_logit, batch, seq):
    n_chunks = seq // RET_CHUNK
    blk = lambda width: pl.BlockSpec((seq, width), lambda b, h: (b, h))
    return pl.pallas_call(
        _ret_core_kernel,
        grid=(batch, RET_HEADS),
        in_specs=[pl.BlockSpec(memory_space=pltpu.SMEM), blk(RET_DK), blk(RET_DK), blk(RET_DV), blk(RET_DV)],
        out_specs=blk(RET_DV),
        out_shape=jax.ShapeDtypeStruct(v.shape, BF16),
        scratch_shapes=[pltpu.VMEM((RET_DK, RET_DV), F32), pltpu.VMEM((RET_DK, RET_DV), F32),
                        pltpu.VMEM((n_chunks, RET_DK, RET_DV), BF16),
                        pltpu.VMEM((n_chunks, RET_DK, RET_DV), BF16)],
        compiler_params=pltpu.CompilerParams(dimension_semantics=("arbitrary", "arbitrary"),
                                             vmem_limit_bytes=VMEM_LIMIT_BYTES),
        name="ret_core",
    )(decay_logit, q, k, v, g)


def _fold_rows(src_ref, sum_ref, dif_ref):
    blk = FOURIER_ROWS
    n = src_ref.shape[0]
    ii = lax.broadcasted_iota(jnp.int32, (blk, blk), 0)
    jj = lax.broadcasted_iota(jnp.int32, (blk, blk), 1)
    flip = jnp.where(ii + jj == blk - 1, 1.0, 0.0).astype(BF16)
    for i in range(n // 2 // blk):
        lo = src_ref[i * blk:(i + 1) * blk, :].astype(F32)
        hi = _dot(flip, src_ref[n - (i + 1) * blk:n - i * blk, :])
        sum_ref[i * blk:(i + 1) * blk, :] = (lo + hi).astype(sum_ref.dtype)
        dif_ref[i * blk:(i + 1) * blk, :] = (lo - hi).astype(dif_ref.dtype)


def _fourier_kernel(ho_ref, heo_ref, hee_ref, qc_ref, qs_ref, pc_ref, ps_ref, pe_ref, wc_ref, ws_ref,
                    f_ref, qsum_ref, qdif_ref, psum_ref, pdif_ref):
    rows = f_ref.shape[1]
    d = ho_ref.shape[1]

    @pl.when(pl.program_id(1) == 0)
    def _():
        _fold_rows(ho_ref, qsum_ref, qdif_ref)
        _fold_rows(heo_ref, psum_ref, pdif_ref)

    def channel_dft(a, b):
        a = a.astype(BF16)
        b = b.astype(BF16)
        return [_dot(a[:, lo:lo + FNO_GROUP_DIM], wc_ref[...]) + _dot(b[:, lo:lo + FNO_GROUP_DIM], ws_ref[...])
                for lo in range(0, d, FNO_GROUP_DIM)]

    qa = _dot(qc_ref[...], qsum_ref[...])
    qb = _dot(qs_ref[...], qdif_ref[...])
    pe = _dot(pe_ref[...], hee_ref[...])
    fq0 = channel_dft(qa[0:rows], qb[0:rows])
    fq1 = channel_dft(qa[rows:2 * rows], qb[rows:2 * rows])
    fpe = channel_dft(pe[0:rows], pe[rows:2 * rows])
    fpo = channel_dft(_dot(pc_ref[...], psum_ref[...]), _dot(ps_ref[...], pdif_ref[...]))
    for grp in range(d // FNO_GROUP_DIM):
        cols = slice(grp * FNO_GROUP_DIM, (grp + 1) * FNO_GROUP_DIM)
        even = fpe[grp] + fpo[grp]
        odd = fpe[grp] - fpo[grp]
        f_ref[0, :, cols] = (even + fq0[grp]).astype(f_ref.dtype)
        f_ref[1, :, cols] = (odd + fq1[grp]).astype(f_ref.dtype)
        f_ref[2, :, cols] = (even - fq0[grp]).astype(f_ref.dtype)
        f_ref[3, :, cols] = (odd - fq1[grp]).astype(f_ref.dtype)


def _fourier_tables(seq):
    n = seq // 4
    rows = FOURIER_ROWS
    tiles = n // rows

    def cis(phase_index):
        phase = (phase_index % seq).astype(F32) * (2.0 * math.pi / seq)
        return jnp.cos(phase), jnp.sin(phase)

    def tiled(x):
        return x.reshape(tiles, rows, x.shape[-1])

    k = jnp.arange(n, dtype=jnp.int32)[:, None]
    j = jnp.arange(n, dtype=jnp.int32)[None, :]
    q_lo = cis(k * (2 * j + 1))
    q_hi = cis((k + n) * (2 * j + 1))
    qc = jnp.concatenate([tiled(q_lo[0]), tiled(q_hi[0])], axis=1).astype(BF16)
    qs = jnp.concatenate([tiled(q_lo[1]), tiled(q_hi[1])], axis=1).astype(BF16)
    po = cis(2 * k * (2 * j[:, :n // 2] + 1))
    pc, ps = tiled(po[0]).astype(BF16), tiled(po[1]).astype(BF16)
    pe = cis(4 * k * j)
    pe = jnp.concatenate([tiled(pe[0]), tiled(pe[1])], axis=1).astype(BF16)

    gdim = FNO_GROUP_DIM
    dd = jnp.arange(gdim, dtype=jnp.int32)
    phase = ((dd[:, None] * dd[None, :]) % gdim).astype(F32) * (2.0 * math.pi / gdim)
    scale = 1.0 / math.sqrt(seq * gdim)
    wc = (jnp.cos(phase) * scale).astype(BF16)
    ws = (-jnp.sin(phase) * scale).astype(BF16)
    return qc, qs, pc, ps, pe, wc, ws


def _fourier(ho, heo, hee, tables, batch, seq):
    d = ho.shape[1]
    n = seq // 4
    rows = FOURIER_ROWS
    qc, qs, pc, ps, pe, wc, ws = tables
    src = lambda nrows: pl.BlockSpec((nrows, d), lambda b, kt: (b, 0))
    tab = lambda t: pl.BlockSpec((None,) + t.shape[1:], lambda b, kt: (kt, 0, 0))
    f = pl.pallas_call(
        _fourier_kernel,
        grid=(batch, n // rows),
        in_specs=[src(2 * n), src(n), src(n), tab(qc), tab(qs), tab(pc), tab(ps), tab(pe),
                  _resident(wc.shape), _resident(ws.shape)],
        out_specs=pl.BlockSpec((None, 4, rows, d), lambda b, kt: (b, 0, kt, 0)),
        out_shape=jax.ShapeDtypeStruct((batch, 4, n, d), BF16),
        scratch_shapes=[pltpu.VMEM((n, d), BF16), pltpu.VMEM((n, d), BF16),
                        pltpu.VMEM((n // 2, d), BF16), pltpu.VMEM((n // 2, d), BF16)],
        compiler_params=pltpu.CompilerParams(dimension_semantics=("arbitrary", "arbitrary"),
                                             vmem_limit_bytes=VMEM_LIMIT_BYTES),
        name="fourier",
    )(ho, heo, hee, qc, qs, pc, ps, pe, wc, ws)
    return f.reshape(batch * seq, d)


def _mid_kernel(x_ref, m_ref, p_ref, mgain_ref, wmix_ref, nffn_ref, wg_ref, wu_ref, wd_ref,
                nple_ref, wpg_ref, wpp_ref, nout_ref, *refs, final, emit_h, scale_m):
    out_ref = refs[0]
    subs = _sub_tiles(x_ref.shape[0])

    def mix_in(rows):
        m = m_ref[rows, :]
        if scale_m:
            m = (m.astype(F32) * mgain_ref[...]).astype(BF16)
        return x_ref[rows, :] + _dot(m, wmix_ref[...])

    def ffn_act(x):
        h = _rmsnorm(x, nffn_ref[...]).astype(BF16)
        return (_silu(_dot(h, wg_ref[...])) * _dot(h, wu_ref[...])).astype(BF16)

    def ple(x, rows):
        hp = _rmsnorm(x, nple_ref[...]).astype(BF16)
        pgate = 0.5 + 0.5 * jnp.tanh(0.5 * _dot(hp, wpg_ref[...]))
        return x + pgate * _dot(p_ref[rows, :].astype(BF16), wpp_ref[...])

    xs = [mix_in(rows) for rows in subs]
    acts = [ffn_act(x) for x in xs]
    xs = [x + _dot(act, wd_ref[...]) for x, act in zip(xs, acts)]
    nrow = SUB_TILE
    xs = [ple(x, rows) for x, rows in zip(xs, subs)]
    for r0, x, rows in zip(range(0, nrow * len(subs), nrow), xs, subs):
        if final:
            out_ref[rows, :] = _rmsnorm(x, nout_ref[...])
            continue
        out_ref[rows, :] = x
        if emit_h:
            ho_ref, heo_ref, hee_ref, hn_ref = refs[1], refs[2], refs[3], refs[4]
            hn = _rmsnorm(x, nout_ref[...])
            for blk in range(hn_ref.shape[0]):
                cols = slice(blk * LANES, (blk + 1) * LANES)
                hn_ref[blk, rows, :] = hn[:, cols]
                ho_ref[r0 // 2:(r0 + nrow) // 2, cols] = (
                    hn_ref[blk, pl.ds(r0 + 1, nrow // 2, stride=2), :].astype(ho_ref.dtype))
                heo_ref[r0 // 4:(r0 + nrow) // 4, cols] = (
                    hn_ref[blk, pl.ds(r0 + 2, nrow // 4, stride=4), :].astype(heo_ref.dtype))
                hee_ref[r0 // 4:(r0 + nrow) // 4, cols] = (
                    hn_ref[blk, pl.ds(r0, nrow // 4, stride=4), :].astype(hee_ref.dtype))


def _mid(x, m, p, layer, mgain, wmix, nffn, wg, wu, wd, nple, wpg, wpp, nout, *, final, emit_h):
    t, d = x.shape
    tm = TOKEN_TILE
    tok = lambda width: pl.BlockSpec((tm, width), lambda i: (i, 0))
    out_specs = [tok(d)]
    out_shape = [jax.ShapeDtypeStruct((t, d), F32)]
    scratch = []
    if emit_h:
        for part in (2, 4, 4):
            out_specs.append(pl.BlockSpec((tm // part, d), lambda i: (i, 0)))
            out_shape.append(jax.ShapeDtypeStruct((t // part, d), BF16))
        scratch = [pltpu.VMEM((d // LANES, tm, LANES), F32)]
    scale_m = mgain is not None
    if mgain is None:
        mgain = jnp.ones((1, m.shape[1]), F32)
    weights = (mgain, wmix, nffn, wg, wu, wd, nple, wpg, wpp, nout)
    p_spec = pl.BlockSpec((None, tm, p.shape[2]), lambda i: (layer, i, 0))
    return pl.pallas_call(
        functools.partial(_mid_kernel, final=final, emit_h=emit_h, scale_m=scale_m),
        grid=(t // tm,),
        in_specs=[tok(d), tok(m.shape[1]), p_spec] + [_resident(w.shape) for w in weights],
        out_specs=out_specs,
        out_shape=out_shape,
        scratch_shapes=scratch,
        compiler_params=pltpu.CompilerParams(dimension_semantics=("arbitrary",),
                                             vmem_limit_bytes=VMEM_LIMIT_BYTES),
        name="mid",
    )(x, m, p, *weights)


def kernel(x, p, positions, norm_mix, ret_w_in, ret_w_out, ret_gn_gain, ret_decay_logit, fno_w_out, norm_ffn, ffn_w_gate, ffn_w_up, ffn_w_down, norm_ple, ple_w_gate, ple_w_proj, final_norm):
    batch, seq, d = x.shape
    depth = p.shape[0]
    t = batch * seq
    xt = x.reshape(t, d)
    pt = p.reshape(depth, t, p.shape[-1])
    pos = positions.reshape(t, 1)
    half = RET_DK // 2
    freq = (1.0 / (ROPE_BASE ** jnp.linspace(0.0, 1.0, half, dtype=F32))).reshape(1, half)
    fourier_tables = _fourier_tables(seq)
    row = lambda v: v.reshape(1, -1)

    h_next = None
    for i in range(depth):
        j = i // 2
        if i % 2 == 0:
            q, k, v, g = _ret_in(xt, pos, freq, row(norm_mix[i]), ret_w_in[j].astype(BF16))
            m = _ret_core(q, k, v, g, ret_decay_logit[j].astype(F32), batch, seq)
            mgain = row(ret_gn_gain[j])
            wmix = ret_w_out[j]
        else:
            m = _fourier(*h_next, fourier_tables, batch, seq)
            mgain = None
            wmix = fno_w_out[j]
        final = i == depth - 1
        emit_h = (not final) and (i + 1) % 2 == 1
        nout = final_norm if final else norm_mix[i + 1]
        outs = _mid(xt, m, pt, i, mgain, wmix.astype(BF16), row(norm_ffn[i]),
                    ffn_w_gate[i].astype(BF16), ffn_w_up[i].astype(BF16), ffn_w_down[i].astype(BF16),
                    row(norm_ple[i]), ple_w_gate[i].astype(BF16), ple_w_proj[i].astype(BF16), row(nout),
                    final=final, emit_h=emit_h)
        xt = outs[0]
        h_next = tuple(outs[1:4]) if emit_h else None
    return xt.reshape(batch, seq, d)
```

```python
import functools
import math

import jax
import jax.numpy as jnp
from jax import lax
from jax.experimental import pallas as pl
from jax.experimental.pallas import tpu as pltpu

F32 = jnp.float32
BF16 = jnp.bfloat16

NORM_EPS = 1e-6
ROPE_BASE = 10000.0
RET_HEADS = 4
RET_DK = 256
RET_DV = 512
FNO_GROUP_DIM = 256
LANES = 128

VMEM_LIMIT_BYTES = 56 * 1024 * 1024

TOKEN_TILE = 512
RET_IN_TILE = 1024
SUB_TILE = 256
RET_CHUNK = 256
FOURIER_ROWS = 256
FLIP_ROWS = 256


def _resident(shape):
    nd = len(shape)
    return pl.BlockSpec(shape, lambda *_: (0,) * nd, pipeline_mode=pl.Buffered(1))


def _layer_resident(stacked, layer):
    nd = stacked.ndim
    return pl.BlockSpec((None,) + stacked.shape[1:], lambda *_: (layer,) + (0,) * (nd - 1),
                        pipeline_mode=pl.Buffered(1))


def _stacked_rows(v):
    return v.reshape(-1, 1, v.shape[-1])


def _rmsnorm(x, gain):
    ms = jnp.mean(x * x, axis=-1, keepdims=True)
    return x * lax.rsqrt(ms + NORM_EPS) * gain


def _silu(x):
    hx = 0.5 * x
    return hx + hx * jnp.tanh(hx)


def _dot(a, b):
    return jnp.dot(a, b, preferred_element_type=F32)


_TWO_OVER_PI = 0.6366197723675814
_PIO2_HI = 1.5703125
_PIO2_MID = 4.837512969970703125e-4
_PIO2_LO = 7.54978995489188216e-8


def _sincos(x):
    n = jnp.floor(x * _TWO_OVER_PI + 0.5)
    r = ((x - n * _PIO2_HI) - n * _PIO2_MID) - n * _PIO2_LO
    r2 = r * r
    s = r + r * r2 * (-1.6666654611e-1 + r2 * (8.3321608736e-3 + r2 * -1.9515295891e-4))
    c = 1.0 - 0.5 * r2 + r2 * r2 * (4.166664568298827e-2 + r2 * (-1.388731625493765e-3 + r2 * 2.443315711809948e-5))
    q = n.astype(jnp.int32)
    odd = (q & 1) == 1
    sin_x = jnp.where(odd, c, s)
    cos_x = jnp.where(odd, s, c)
    sin_x = jnp.where((q & 2) == 2, -sin_x, sin_x)
    cos_x = jnp.where(((q + 1) & 2) == 2, -cos_x, cos_x)
    return sin_x, cos_x


def _sub_tiles(rows):
    return [pl.ds(r, SUB_TILE) for r in range(0, rows, SUB_TILE)]


def _ret_in_kernel(x_ref, pos_ref, freq_ref, gain_ref, w_ref, q_ref, k_ref, v_ref, g_ref):
    qk = RET_HEADS * RET_DK
    vw = RET_HEADS * RET_DV
    half = RET_DK // 2

    subs = _sub_tiles(x_ref.shape[0])
    hs = [_rmsnorm(x_ref[rows, :], gain_ref[...]).astype(BF16) for rows in subs]

    angs = [pos_ref[rows, :].astype(F32) * freq_ref[...] for rows in subs]
    trig = [_sincos(ang) for ang in angs]

    def rope_store(proj, sin_cos, rows, out_ref, scale):
        sin, cos = sin_cos
        for hd in range(RET_HEADS):
            lo = hd * RET_DK
            x1 = proj[:, lo:lo + half]
            x2 = proj[:, lo + half:lo + RET_DK]
            out_ref[rows, lo:lo + half] = ((x1 * cos - x2 * sin) * scale).astype(out_ref.dtype)
            out_ref[rows, lo + half:lo + RET_DK] = ((x1 * sin + x2 * cos) * scale).astype(out_ref.dtype)

    for h, rows in zip(hs, subs):
        v_ref[rows, :] = _dot(h, w_ref[:, 2 * qk:2 * qk + vw]).astype(v_ref.dtype)
    for h, sin_cos, rows in zip(hs, trig, subs):
        rope_store(_dot(h, w_ref[:, 0:qk]), sin_cos, rows, q_ref, 1.0)
    for h, sin_cos, rows in zip(hs, trig, subs):
        rope_store(_dot(h, w_ref[:, qk:2 * qk]), sin_cos, rows, k_ref, RET_DK ** -0.5)
    for h, rows in zip(hs, subs):
        g_ref[rows, :] = _dot(h, w_ref[:, 2 * qk + vw:2 * qk + 2 * vw]).astype(g_ref.dtype)


def _ret_in(x, pos, freq, gain, w_in):
    t, d = x.shape
    tm = RET_IN_TILE
    qk = RET_HEADS * RET_DK
    vw = RET_HEADS * RET_DV
    tok = lambda width: pl.BlockSpec((tm, width), lambda i: (i, 0))
    (gain, gain_layer), (w_in, w_layer) = gain, w_in
    return pl.pallas_call(
        _ret_in_kernel,
        grid=(t // tm,),
        in_specs=[tok(d), tok(1), _resident(freq.shape), _layer_resident(gain, gain_layer),
                  _layer_resident(w_in, w_layer)],
        out_specs=[tok(qk), tok(qk), tok(vw), tok(vw)],
        out_shape=[jax.ShapeDtypeStruct((t, qk), BF16), jax.ShapeDtypeStruct((t, qk), BF16),
                   jax.ShapeDtypeStruct((t, vw), BF16), jax.ShapeDtypeStruct((t, vw), BF16)],
        compiler_params=pltpu.CompilerParams(dimension_semantics=("arbitrary",),
                                             vmem_limit_bytes=VMEM_LIMIT_BYTES),
        name="ret_in",
    )(x, pos, freq, gain, w_in)


def _log_sigmoid(x):
    return jnp.minimum(x, 0.0) - jnp.log(1.0 + jnp.exp(-jnp.abs(x)))


def _ret_core_kernel(logit_ref, q_ref, k_ref, v_ref, g_ref, o_ref, rf_ref, rb_ref, rfs_ref, rbs_ref):
    c_len = RET_CHUNK
    n_chunks = q_ref.shape[0] // c_len
    head = pl.program_id(1)

    def log_gamma(direction, shape):
        return _log_sigmoid(jnp.full(shape, logit_ref[direction, head], F32))

    row = lax.broadcasted_iota(jnp.int32, (c_len, RET_DK), 0).astype(F32)
    lgf = log_gamma(0, (c_len, RET_DK))
    lgb = log_gamma(1, (c_len, RET_DK))
    xi_f = jnp.exp(lgf * (row + 1.0)).astype(BF16)
    zeta_f = jnp.exp(lgf * (c_len - 1.0 - row)).astype(BF16)
    xi_b = jnp.exp(lgb * (c_len - row)).astype(BF16)
    zeta_b = jnp.exp(lgb * row).astype(BF16)
    gchunk_f = jnp.exp(log_gamma(0, (1, 1)) * float(c_len))
    gchunk_b = jnp.exp(log_gamma(1, (1, 1)) * float(c_len))

    ii = lax.broadcasted_iota(jnp.int32, (c_len, c_len), 0)
    jj = lax.broadcasted_iota(jnp.int32, (c_len, c_len), 1)
    dist = (ii - jj).astype(F32)
    decay = jnp.where(ii >= jj,
                      jnp.exp(log_gamma(0, (c_len, c_len)) * jnp.maximum(dist, 0.0)),
                      jnp.exp(log_gamma(1, (c_len, c_len)) * jnp.maximum(-dist, 0.0)))

    def chunk(ref, c):
        return ref[pl.ds(pl.multiple_of(c * c_len, c_len), c_len), :]

    def kt_v(c, zeta):
        return lax.dot_general(chunk(k_ref, c) * zeta, chunk(v_ref, c), (((0,), (0,)), ((), ())),
                               preferred_element_type=F32)

    rf_ref[...] = jnp.zeros_like(rf_ref)
    rb_ref[...] = jnp.zeros_like(rb_ref)
    rfs_ref[0] = jnp.zeros(rfs_ref.shape[1:], rfs_ref.dtype)
    rbs_ref[n_chunks - 1] = jnp.zeros(rbs_ref.shape[1:], rbs_ref.dtype)

    def state_step(i, carry):
        cb = n_chunks - 1 - i
        rf = gchunk_f * rf_ref[...] + kt_v(i, zeta_f)
        rf_ref[...] = rf
        rfs_ref[i + 1] = rf.astype(rfs_ref.dtype)
        rb = gchunk_b * rb_ref[...] + kt_v(cb, zeta_b)
        rb_ref[...] = rb
        rbs_ref[cb - 1] = rb.astype(rbs_ref.dtype)
        return carry

    lax.fori_loop(0, n_chunks - 1, state_step, 0, unroll=True)

    def retention(c):
        q_c = chunk(q_ref, c)
        scores = lax.dot_general(q_c, chunk(k_ref, c), (((1,), (1,)), ((), ())), preferred_element_type=F32)
        return (_dot((scores * decay).astype(BF16), chunk(v_ref, c))
                + _dot(q_c * xi_f, rfs_ref[c]) + _dot(q_c * xi_b, rbs_ref[c]))

    def norm_gate_store(c, o):
        mu = jnp.mean(o, axis=-1, keepdims=True)
        cen = o - mu
        var = jnp.mean(cen * cen, axis=-1, keepdims=True)
        normed = (cen * lax.rsqrt(var + NORM_EPS)).astype(BF16)
        o_ref[pl.ds(pl.multiple_of(c * c_len, c_len), c_len), :] = _silu(chunk(g_ref, c)) * normed

    def out_step(c, carry):
        norm_gate_store(c, retention(c))
        return carry

    lax.fori_loop(0, n_chunks, out_step, 0, unroll=True)


def _ret_core(q, k, v, g, decay_logit, batch, seq):
    n_chunks = seq // RET_CHUNK
    blk = lambda width: pl.BlockSpec((seq, width), lambda b, h: (b, h))
    return pl.pallas_call(
        _ret_core_kernel,
        grid=(batch, RET_HEADS),
        in_specs=[pl.BlockSpec(memory_space=pltpu.SMEM), blk(RET_DK), blk(RET_DK), blk(RET_DV), blk(RET_DV)],
        out_specs=blk(RET_DV),
        out_shape=jax.ShapeDtypeStruct(v.shape, BF16),
        scratch_shapes=[pltpu.VMEM((RET_DK, RET_DV), F32), pltpu.VMEM((RET_DK, RET_DV), F32),
                        pltpu.VMEM((n_chunks, RET_DK, RET_DV), BF16),
                        pltpu.VMEM((n_chunks, RET_DK, RET_DV), BF16)],
        compiler_params=pltpu.CompilerParams(dimension_semantics=("arbitrary", "arbitrary"),
                                             vmem_limit_bytes=VMEM_LIMIT_BYTES),
        name="ret_core",
    )(decay_logit, q, k, v, g)


def _fold_rows(src_ref, sum_ref, dif_ref):
    blk = FLIP_ROWS
    n = src_ref.shape[0]
    ii = lax.broadcasted_iota(jnp.int32, (blk, blk), 0)
    jj = lax.broadcasted_iota(jnp.int32, (blk, blk), 1)
    flip = jnp.where(ii + jj == blk - 1, 1.0, 0.0).astype(BF16)
    for i in range(n // 2 // blk):
        lo = src_ref[i * blk:(i + 1) * blk, :].astype(F32)
        hi = _dot(flip, src_ref[n - (i + 1) * blk:n - i * blk, :])
        sum_ref[i * blk:(i + 1) * blk, :] = (lo + hi).astype(sum_ref.dtype)
        dif_ref[i * blk:(i + 1) * blk, :] = (lo - hi).astype(dif_ref.dtype)


def _fourier_kernel(ho_ref, heo_ref, hee_ref, qc_ref, qs_ref, pc_ref, ps_ref, pe_ref, wc_ref, ws_ref,
                    f_ref, qsum_ref, qdif_ref, psum_ref, pdif_ref):
    rows = f_ref.shape[1]
    d = ho_ref.shape[1]

    @pl.when(pl.program_id(1) == 0)
    def _():
        _fold_rows(ho_ref, qsum_ref, qdif_ref)
        _fold_rows(heo_ref, psum_ref, pdif_ref)

    def channel_dft(a, b):
        a = a.astype(BF16)
        b = b.astype(BF16)
        return [_dot(a[:, lo:lo + FNO_GROUP_DIM], wc_ref[...]) + _dot(b[:, lo:lo + FNO_GROUP_DIM], ws_ref[...])
                for lo in range(0, d, FNO_GROUP_DIM)]

    qa = _dot(qc_ref[...], qsum_ref[...])
    qb = _dot(qs_ref[...], qdif_ref[...])
    pe = _dot(pe_ref[...], hee_ref[...])
    fq0 = channel_dft(qa[0:rows], qb[0:rows])
    fq1 = channel_dft(qa[rows:2 * rows], qb[rows:2 * rows])
    fpe = channel_dft(pe[0:rows], pe[rows:2 * rows])
    fpo = channel_dft(_dot(pc_ref[...], psum_ref[...]), _dot(ps_ref[...], pdif_ref[...]))
    for grp in range(d // FNO_GROUP_DIM):
        cols = slice(grp * FNO_GROUP_DIM, (grp + 1) * FNO_GROUP_DIM)
        even = fpe[grp] + fpo[grp]
        odd = fpe[grp] - fpo[grp]
        f_ref[0, :, cols] = (even + fq0[grp]).astype(f_ref.dtype)
        f_ref[1, :, cols] = (odd + fq1[grp]).astype(f_ref.dtype)
        f_ref[2, :, cols] = (even - fq0[grp]).astype(f_ref.dtype)
        f_ref[3, :, cols] = (odd - fq1[grp]).astype(f_ref.dtype)


def _fourier_tables(seq):
    n = seq // 4
    rows = FOURIER_ROWS

    def cis(phase_index):
        phase = (phase_index % seq).astype(F32) * (2.0 * math.pi / seq)
        return jnp.cos(phase), jnp.sin(phase)

    def cis_table(alpha, n_rows, b):
        step = 32
        kh = jnp.arange(n_rows // step, dtype=jnp.int32)[:, None]
        kl = jnp.arange(step, dtype=jnp.int32)[:, None]
        ch, sh = cis(alpha * step * kh * b[None, :])
        cl, sl = cis(alpha * kl * b[None, :])
        ch, sh, cl, sl = ch[:, None, :], sh[:, None, :], cl[None], sl[None]
        return (ch * cl - sh * sl).reshape(n_rows, -1), (sh * cl + ch * sl).reshape(n_rows, -1)

    def tiled(x):
        return x.reshape(-1, rows, x.shape[-1])

    j = jnp.arange(n, dtype=jnp.int32)
    q_cos, q_sin = cis_table(1, 2 * n, 2 * j + 1)
    qc = jnp.concatenate([tiled(q_cos[:n]), tiled(q_cos[n:])], axis=1).astype(BF16)
    qs = jnp.concatenate([tiled(q_sin[:n]), tiled(q_sin[n:])], axis=1).astype(BF16)
    po = cis_table(2, n, 2 * j[:n // 2] + 1)
    pc, ps = tiled(po[0]).astype(BF16), tiled(po[1]).astype(BF16)
    pe = cis_table(4, n, j)
    pe = jnp.concatenate([tiled(pe[0]), tiled(pe[1])], axis=1).astype(BF16)

    gdim = FNO_GROUP_DIM
    dd = jnp.arange(gdim, dtype=jnp.int32)
    phase = ((dd[:, None] * dd[None, :]) % gdim).astype(F32) * (2.0 * math.pi / gdim)
    scale = 1.0 / math.sqrt(seq * gdim)
    wc = (jnp.cos(phase) * scale).astype(BF16)
    ws = (-jnp.sin(phase) * scale).astype(BF16)
    return qc, qs, pc, ps, pe, wc, ws


def _fourier(ho, heo, hee, tables, batch, seq):
    d = ho.shape[1]
    n = seq // 4
    rows = FOURIER_ROWS
    qc, qs, pc, ps, pe, wc, ws = tables
    src = lambda nrows: pl.BlockSpec((nrows, d), lambda b, kt: (b, 0))
    tab = lambda t: pl.BlockSpec((None,) + t.shape[1:], lambda b, kt: (kt, 0, 0))
    f = pl.pallas_call(
        _fourier_kernel,
        grid=(batch, n // rows),
        in_specs=[src(2 * n), src(n), src(n), tab(qc), tab(qs), tab(pc), tab(ps), tab(pe),
                  _resident(wc.shape), _resident(ws.shape)],
        out_specs=pl.BlockSpec((None, 4, rows, d), lambda b, kt: (b, 0, kt, 0)),
        out_shape=jax.ShapeDtypeStruct((batch, 4, n, d), BF16),
        scratch_shapes=[pltpu.VMEM((n, d), BF16), pltpu.VMEM((n, d), BF16),
                        pltpu.VMEM((n // 2, d), BF16), pltpu.VMEM((n // 2, d), BF16)],
        compiler_params=pltpu.CompilerParams(dimension_semantics=("arbitrary", "arbitrary"),
                                             vmem_limit_bytes=VMEM_LIMIT_BYTES),
        name="fourier",
    )(ho, heo, hee, qc, qs, pc, ps, pe, wc, ws)
    return f.reshape(batch * seq, d)


def _mid_kernel(x_ref, m_ref, p_ref, mgain_ref, wmix_ref, nffn_ref, wg_ref, wu_ref, wd_ref,
                nple_ref, wpg_ref, wpp_ref, nout_ref, *refs, final, emit_h, scale_m):
    out_ref = refs[0]
    subs = _sub_tiles(x_ref.shape[0])

    def mix_in(rows):
        m = m_ref[rows, :]
        if scale_m:
            m = (m.astype(F32) * mgain_ref[...]).astype(BF16)
        return x_ref[rows, :] + _dot(m, wmix_ref[...])

    def ffn_act(x):
        h = _rmsnorm(x, nffn_ref[...]).astype(BF16)
        return (_silu(_dot(h, wg_ref[...])) * _dot(h, wu_ref[...])).astype(BF16)

    def ple(x, rows):
        hp = _rmsnorm(x, nple_ref[...]).astype(BF16)
        pgate = 0.5 + 0.5 * jnp.tanh(0.5 * _dot(hp, wpg_ref[...]))
        return x + pgate * _dot(p_ref[rows, :].astype(BF16), wpp_ref[...])

    xs = [mix_in(rows) for rows in subs]
    acts = [ffn_act(x) for x in xs]
    xs = [x + _dot(act, wd_ref[...]) for x, act in zip(xs, acts)]
    nrow = SUB_TILE
    xs = [ple(x, rows) for x, rows in zip(xs, subs)]
    for r0, x, rows in zip(range(0, nrow * len(subs), nrow), xs, subs):
        if final:
            out_ref[rows, :] = _rmsnorm(x, nout_ref[...])
            continue
        out_ref[rows, :] = x
        if emit_h:
            ho_ref, heo_ref, hee_ref, hn_ref = refs[1], refs[2], refs[3], refs[4]
            hn = _rmsnorm(x, nout_ref[...])
            for blk in range(hn_ref.shape[0]):
                cols = slice(blk * LANES, (blk + 1) * LANES)
                hn_ref[blk, rows, :] = hn[:, cols]
                ho_ref[r0 // 2:(r0 + nrow) // 2, cols] = (
                    hn_ref[blk, pl.ds(r0 + 1, nrow // 2, stride=2), :].astype(ho_ref.dtype))
                heo_ref[r0 // 4:(r0 + nrow) // 4, cols] = (
                    hn_ref[blk, pl.ds(r0 + 2, nrow // 4, stride=4), :].astype(heo_ref.dtype))
                hee_ref[r0 // 4:(r0 + nrow) // 4, cols] = (
                    hn_ref[blk, pl.ds(r0, nrow // 4, stride=4), :].astype(hee_ref.dtype))


def _mid(x, m, p, layer, mgain, wmix, nffn, wg, wu, wd, nple, wpg, wpp, nout, *, final, emit_h):
    t, d = x.shape
    tm = TOKEN_TILE
    tok = lambda width: pl.BlockSpec((tm, width), lambda i: (i, 0))
    out_specs = [tok(d)]
    out_shape = [jax.ShapeDtypeStruct((t, d), F32)]
    scratch = []
    if emit_h:
        for part in (2, 4, 4):
            out_specs.append(pl.BlockSpec((tm // part, d), lambda i: (i, 0)))
            out_shape.append(jax.ShapeDtypeStruct((t // part, d), BF16))
        scratch = [pltpu.VMEM((d // LANES, tm, LANES), F32)]
    scale_m = mgain is not None
    if mgain is None:
        mgain = (jnp.ones((1, 1, m.shape[1]), F32), 0)
    params = (mgain, wmix, nffn, wg, wu, wd, nple, wpg, wpp, nout)
    weights = [w for w, _ in params]
    p_spec = pl.BlockSpec((None, tm, p.shape[2]), lambda i: (layer, i, 0))
    return pl.pallas_call(
        functools.partial(_mid_kernel, final=final, emit_h=emit_h, scale_m=scale_m),
        grid=(t // tm,),
        in_specs=[tok(d), tok(m.shape[1]), p_spec] + [_layer_resident(w, idx) for w, idx in params],
        out_specs=out_specs,
        out_shape=out_shape,
        scratch_shapes=scratch,
        compiler_params=pltpu.CompilerParams(dimension_semantics=("arbitrary",),
                                             vmem_limit_bytes=VMEM_LIMIT_BYTES),
        name="mid",
    )(x, m, p, *weights)


def kernel(x, p, positions, norm_mix, ret_w_in, ret_w_out, ret_gn_gain, ret_decay_logit, fno_w_out, norm_ffn, ffn_w_gate, ffn_w_up, ffn_w_down, norm_ple, ple_w_gate, ple_w_proj, final_norm):
    batch, seq, d = x.shape
    depth = p.shape[0]
    t = batch * seq
    xt = x.reshape(t, d)
    pt = p.reshape(depth, t, p.shape[-1])
    pos = positions.reshape(t, 1)
    half = RET_DK // 2
    freq = (1.0 / (ROPE_BASE ** jnp.linspace(0.0, 1.0, half, dtype=F32))).reshape(1, half)
    fourier_tables = _fourier_tables(seq)

    bf = lambda w: w.astype(BF16)
    ret_w_in, ret_w_out, fno_w_out = bf(ret_w_in), bf(ret_w_out), bf(fno_w_out)
    ffn_w_gate, ffn_w_up, ffn_w_down = bf(ffn_w_gate), bf(ffn_w_up), bf(ffn_w_down)
    ple_w_gate, ple_w_proj = bf(ple_w_gate), bf(ple_w_proj)
    norm_mix, norm_ffn, norm_ple = _stacked_rows(norm_mix), _stacked_rows(norm_ffn), _stacked_rows(norm_ple)
    final_norm, ret_gn_gain = _stacked_rows(final_norm), _stacked_rows(ret_gn_gain)

    h_next = None
    for i in range(depth):
        j = i // 2
        if i % 2 == 0:
            q, k, v, g = _ret_in(xt, pos, freq, (norm_mix, i), (ret_w_in, j))
            m = _ret_core(q, k, v, g, ret_decay_logit[j].astype(F32), batch, seq)
            mgain = (ret_gn_gain, j)
            wmix = (ret_w_out, j)
        else:
            m = _fourier(*h_next, fourier_tables, batch, seq)
            mgain = None
            wmix = (fno_w_out, j)
        final = i == depth - 1
        emit_h = (not final) and (i + 1) % 2 == 1
        nout = (final_norm, 0) if final else (norm_mix, i + 1)
        outs = _mid(xt, m, pt, i, mgain, wmix, (norm_ffn, i), (ffn_w_gate, i), (ffn_w_up, i), (ffn_w_down, i),
                    (norm_ple, i), (ple_w_gate, i), (ple_w_proj, i), nout, final=final, emit_h=emit_h)
        xt = outs[0]
        h_next = tuple(outs[1:4]) if emit_h else None
    return xt.reshape(batch, seq, d)
```

```python
import functools
import math

import jax
import jax.numpy as jnp
from jax import lax
from jax.experimental import pallas as pl
from jax.experimental.pallas import tpu as pltpu

F32 = jnp.float32
BF16 = jnp.bfloat16

NORM_EPS = 1e-6
ROPE_BASE = 10000.0
RET_HEADS = 4
RET_DK = 256
RET_DV = 512
FNO_GROUP_DIM = 256
LANES = 128
MXU_COLS = 256

VMEM_LIMIT_BYTES = 56 * 1024 * 1024

TOKEN_TILE = 512
RET_IN_TILE = 1024
SUB_TILE = 256
RET_CHUNK = 256
FOURIER_ROWS = 256
FLIP_ROWS = 256


def _resident(shape):
    nd = len(shape)
    return pl.BlockSpec(shape, lambda *_: (0,) * nd, pipeline_mode=pl.Buffered(1))


def _layer_resident(stacked, layer):
    nd = stacked.ndim
    return pl.BlockSpec((None,) + stacked.shape[1:], lambda *_: (layer,) + (0,) * (nd - 1),
                        pipeline_mode=pl.Buffered(1))


def _stacked_rows(v):
    return v.reshape(-1, 1, v.shape[-1])


def _rmsnorm(x, gain):
    ms = jnp.mean(x * x, axis=-1, keepdims=True)
    return x * lax.rsqrt(ms + NORM_EPS) * gain


def _silu(x):
    hx = 0.5 * x
    return hx + hx * jnp.tanh(hx)


def _dot(a, b):
    return jnp.dot(a, b, preferred_element_type=F32)


_TWO_OVER_PI = 0.6366197723675814
_PIO2_HI = 1.5703125
_PIO2_MID = 4.837512969970703125e-4
_PIO2_LO = 7.54978995489188216e-8


def _sincos(x):
    n = jnp.floor(x * _TWO_OVER_PI + 0.5)
    r = ((x - n * _PIO2_HI) - n * _PIO2_MID) - n * _PIO2_LO
    r2 = r * r
    s = r + r * r2 * (-1.6666654611e-1 + r2 * (8.3321608736e-3 + r2 * -1.9515295891e-4))
    c = 1.0 - 0.5 * r2 + r2 * r2 * (4.166664568298827e-2 + r2 * (-1.388731625493765e-3 + r2 * 2.443315711809948e-5))
    q = n.astype(jnp.int32)
    odd = (q & 1) == 1
    sin_x = jnp.where(odd, c, s)
    cos_x = jnp.where(odd, s, c)
    sin_x = jnp.where((q & 2) == 2, -sin_x, sin_x)
    cos_x = jnp.where(((q + 1) & 2) == 2, -cos_x, cos_x)
    return sin_x, cos_x


def _sub_tiles(rows):
    return [pl.ds(r, SUB_TILE) for r in range(0, rows, SUB_TILE)]


def _ret_in_kernel(x_ref, pos_ref, freq_ref, gain_ref, w_ref, q_ref, k_ref, v_ref, g_ref):
    qk = RET_HEADS * RET_DK
    vw = RET_HEADS * RET_DV
    half = RET_DK // 2

    subs = _sub_tiles(x_ref.shape[0])
    hs = [_rmsnorm(x_ref[rows, :], gain_ref[...]).astype(BF16) for rows in subs]

    angs = [pos_ref[rows, :].astype(F32) * freq_ref[...] for rows in subs]
    trig = [_sincos(ang) for ang in angs]

    def rope_store(proj, sin_cos, rows, out_ref, scale):
        sin, cos = sin_cos
        for hd in range(RET_HEADS):
            lo = hd * RET_DK
            x1 = proj[:, lo:lo + half]
            x2 = proj[:, lo + half:lo + RET_DK]
            out_ref[rows, lo:lo + half] = ((x1 * cos - x2 * sin) * scale).astype(out_ref.dtype)
            out_ref[rows, lo + half:lo + RET_DK] = ((x1 * sin + x2 * cos) * scale).astype(out_ref.dtype)

    for h, rows in zip(hs, subs):
        v_ref[rows, :] = _dot(h, w_ref[:, 2 * qk:2 * qk + vw]).astype(v_ref.dtype)
    for h, sin_cos, rows in zip(hs, trig, subs):
        rope_store(_dot(h, w_ref[:, 0:qk]), sin_cos, rows, q_ref, 1.0)
    for h, sin_cos, rows in zip(hs, trig, subs):
        rope_store(_dot(h, w_ref[:, qk:2 * qk]), sin_cos, rows, k_ref, RET_DK ** -0.5)
    for h, rows in zip(hs, subs):
        g_ref[rows, :] = _dot(h, w_ref[:, 2 * qk + vw:2 * qk + 2 * vw]).astype(g_ref.dtype)


def _ret_in(x, pos, freq, gain, w_in):
    t, d = x.shape
    tm = RET_IN_TILE
    qk = RET_HEADS * RET_DK
    vw = RET_HEADS * RET_DV
    tok = lambda width: pl.BlockSpec((tm, width), lambda i: (i, 0))
    (gain, gain_layer), (w_in, w_layer) = gain, w_in
    return pl.pallas_call(
        _ret_in_kernel,
        grid=(t // tm,),
        in_specs=[tok(d), tok(1), _resident(freq.shape), _layer_resident(gain, gain_layer),
                  _layer_resident(w_in, w_layer)],
        out_specs=[tok(qk), tok(qk), tok(vw), tok(vw)],
        out_shape=[jax.ShapeDtypeStruct((t, qk), BF16), jax.ShapeDtypeStruct((t, qk), BF16),
                   jax.ShapeDtypeStruct((t, vw), BF16), jax.ShapeDtypeStruct((t, vw), BF16)],
        compiler_params=pltpu.CompilerParams(dimension_semantics=("arbitrary",),
                                             vmem_limit_bytes=VMEM_LIMIT_BYTES),
        name="ret_in",
    )(x, pos, freq, gain, w_in)


def _log_sigmoid(x):
    return jnp.minimum(x, 0.0) - jnp.log(1.0 + jnp.exp(-jnp.abs(x)))


def _ret_core_kernel(logit_ref, q_ref, k_ref, v_ref, g_ref, o_ref, rf_ref, rb_ref, rfs_ref, rbs_ref):
    c_len = RET_CHUNK
    n_chunks = q_ref.shape[0] // c_len
    head = pl.program_id(1)

    def log_gamma(direction, shape):
        return _log_sigmoid(jnp.full(shape, logit_ref[direction, head], F32))

    row = lax.broadcasted_iota(jnp.int32, (c_len, RET_DK), 0).astype(F32)
    lgf = log_gamma(0, (c_len, RET_DK))
    lgb = log_gamma(1, (c_len, RET_DK))
    xi_f = jnp.exp(lgf * (row + 1.0)).astype(BF16)
    zeta_f = jnp.exp(lgf * (c_len - 1.0 - row)).astype(BF16)
    xi_b = jnp.exp(lgb * (c_len - row)).astype(BF16)
    zeta_b = jnp.exp(lgb * row).astype(BF16)
    gchunk_f = jnp.exp(log_gamma(0, (1, 1)) * float(c_len))
    gchunk_b = jnp.exp(log_gamma(1, (1, 1)) * float(c_len))

    ii = lax.broadcasted_iota(jnp.int32, (c_len, c_len), 0)
    jj = lax.broadcasted_iota(jnp.int32, (c_len, c_len), 1)
    dist = (ii - jj).astype(F32)
    decay = jnp.where(ii >= jj,
                      jnp.exp(log_gamma(0, (c_len, c_len)) * jnp.maximum(dist, 0.0)),
                      jnp.exp(log_gamma(1, (c_len, c_len)) * jnp.maximum(-dist, 0.0)))

    def chunk(ref, c):
        return ref[pl.ds(pl.multiple_of(c * c_len, c_len), c_len), :]

    def kt_v(c, zeta):
        return lax.dot_general(chunk(k_ref, c) * zeta, chunk(v_ref, c), (((0,), (0,)), ((), ())),
                               preferred_element_type=F32)

    rf_ref[...] = jnp.zeros_like(rf_ref)
    rb_ref[...] = jnp.zeros_like(rb_ref)
    rfs_ref[0] = jnp.zeros(rfs_ref.shape[1:], rfs_ref.dtype)
    rbs_ref[n_chunks - 1] = jnp.zeros(rbs_ref.shape[1:], rbs_ref.dtype)

    def state_step(i, carry):
        cb = n_chunks - 1 - i
        rf = gchunk_f * rf_ref[...] + kt_v(i, zeta_f)
        rf_ref[...] = rf
        rfs_ref[i + 1] = rf.astype(rfs_ref.dtype)
        rb = gchunk_b * rb_ref[...] + kt_v(cb, zeta_b)
        rb_ref[...] = rb
        rbs_ref[cb - 1] = rb.astype(rbs_ref.dtype)
        return carry

    lax.fori_loop(0, n_chunks - 1, state_step, 0, unroll=True)

    def retention(c):
        q_c = chunk(q_ref, c)
        scores = lax.dot_general(q_c, chunk(k_ref, c), (((1,), (1,)), ((), ())), preferred_element_type=F32)
        return (_dot((scores * decay).astype(BF16), chunk(v_ref, c))
                + _dot(q_c * xi_f, rfs_ref[c]) + _dot(q_c * xi_b, rbs_ref[c]))

    def norm_gate_store(c, o):
        mu = jnp.mean(o, axis=-1, keepdims=True)
        cen = o - mu
        var = jnp.mean(cen * cen, axis=-1, keepdims=True)
        normed = (cen * lax.rsqrt(var + NORM_EPS)).astype(BF16)
        o_ref[pl.ds(pl.multiple_of(c * c_len, c_len), c_len), :] = _silu(chunk(g_ref, c)) * normed

    def out_step(c, carry):
        norm_gate_store(c, retention(c))
        return carry

    lax.fori_loop(0, n_chunks, out_step, 0, unroll=True)


def _ret_core(q, k, v, g, decay_logit, batch, seq):
    n_chunks = seq // RET_CHUNK
    blk = lambda width: pl.BlockSpec((seq, width), lambda b, h: (b, h))
    return pl.pallas_call(
        _ret_core_kernel,
        grid=(batch, RET_HEADS),
        in_specs=[pl.BlockSpec(memory_space=pltpu.SMEM), blk(RET_DK), blk(RET_DK), blk(RET_DV), blk(RET_DV)],
        out_specs=blk(RET_DV),
        out_shape=jax.ShapeDtypeStruct(v.shape, BF16),
        scratch_shapes=[pltpu.VMEM((RET_DK, RET_DV), F32), pltpu.VMEM((RET_DK, RET_DV), F32),
                        pltpu.VMEM((n_chunks, RET_DK, RET_DV), BF16),
                        pltpu.VMEM((n_chunks, RET_DK, RET_DV), BF16)],
        compiler_params=pltpu.CompilerParams(dimension_semantics=("arbitrary", "arbitrary"),
                                             vmem_limit_bytes=VMEM_LIMIT_BYTES),
        name="ret_core",
    )(decay_logit, q, k, v, g)


def _fold_rows(src_ref, sum_ref, dif_ref):
    blk = FLIP_ROWS
    n = src_ref.shape[0]
    ii = lax.broadcasted_iota(jnp.int32, (blk, blk), 0)
    jj = lax.broadcasted_iota(jnp.int32, (blk, blk), 1)
    flip = jnp.where(ii + jj == blk - 1, 1.0, 0.0).astype(BF16)
    for i in range(n // 2 // blk):
        lo = src_ref[i * blk:(i + 1) * blk, :].astype(F32)
        hi = _dot(flip, src_ref[n - (i + 1) * blk:n - i * blk, :])
        sum_ref[i * blk:(i + 1) * blk, :] = (lo + hi).astype(sum_ref.dtype)
        dif_ref[i * blk:(i + 1) * blk, :] = (lo - hi).astype(dif_ref.dtype)


def _fourier_kernel(ho_ref, heo_ref, hee_ref, qc_ref, qs_ref, pc_ref, ps_ref, pe_ref, wc_ref, ws_ref,
                    f_ref, qsum_ref, qdif_ref, psum_ref, pdif_ref):
    rows = f_ref.shape[1]
    d = ho_ref.shape[1]

    @pl.when(pl.program_id(1) == 0)
    def _():
        _fold_rows(ho_ref, qsum_ref, qdif_ref)
        _fold_rows(heo_ref, psum_ref, pdif_ref)

    def channel_dft(a, b):
        a = a.astype(BF16)
        b = b.astype(BF16)
        return [_dot(a[:, lo:lo + FNO_GROUP_DIM], wc_ref[...]) + _dot(b[:, lo:lo + FNO_GROUP_DIM], ws_ref[...])
                for lo in range(0, d, FNO_GROUP_DIM)]

    qa = _dot(qc_ref[...], qsum_ref[...])
    qb = _dot(qs_ref[...], qdif_ref[...])
    pe = _dot(pe_ref[...], hee_ref[...])
    fq0 = channel_dft(qa[0:rows], qb[0:rows])
    fq1 = channel_dft(qa[rows:2 * rows], qb[rows:2 * rows])
    fpe = channel_dft(pe[0:rows], pe[rows:2 * rows])
    fpo = channel_dft(_dot(pc_ref[...], psum_ref[...]), _dot(ps_ref[...], pdif_ref[...]))
    for grp in range(d // FNO_GROUP_DIM):
        cols = slice(grp * FNO_GROUP_DIM, (grp + 1) * FNO_GROUP_DIM)
        even = fpe[grp] + fpo[grp]
        odd = fpe[grp] - fpo[grp]
        f_ref[0, :, cols] = (even + fq0[grp]).astype(f_ref.dtype)
        f_ref[1, :, cols] = (odd + fq1[grp]).astype(f_ref.dtype)
        f_ref[2, :, cols] = (even - fq0[grp]).astype(f_ref.dtype)
        f_ref[3, :, cols] = (odd - fq1[grp]).astype(f_ref.dtype)


def _fourier_tables(seq):
    n = seq // 4
    rows = FOURIER_ROWS

    def cis(phase_index):
        phase = (phase_index % seq).astype(F32) * (2.0 * math.pi / seq)
        return jnp.cos(phase), jnp.sin(phase)

    def cis_table(alpha, n_rows, b):
        step = 32
        kh = jnp.arange(n_rows // step, dtype=jnp.int32)[:, None]
        kl = jnp.arange(step, dtype=jnp.int32)[:, None]
        ch, sh = cis(alpha * step * kh * b[None, :])
        cl, sl = cis(alpha * kl * b[None, :])
        ch, sh, cl, sl = ch[:, None, :], sh[:, None, :], cl[None], sl[None]
        return (ch * cl - sh * sl).reshape(n_rows, -1), (sh * cl + ch * sl).reshape(n_rows, -1)

    def tiled(x):
        return x.reshape(-1, rows, x.shape[-1])

    j = jnp.arange(n, dtype=jnp.int32)
    q_cos, q_sin = cis_table(1, 2 * n, 2 * j + 1)
    qc = jnp.concatenate([tiled(q_cos[:n]), tiled(q_cos[n:])], axis=1).astype(BF16)
    qs = jnp.concatenate([tiled(q_sin[:n]), tiled(q_sin[n:])], axis=1).astype(BF16)
    po = cis_table(2, n, 2 * j[:n // 2] + 1)
    pc, ps = tiled(po[0]).astype(BF16), tiled(po[1]).astype(BF16)
    pe = cis_table(4, n, j)
    pe = jnp.concatenate([tiled(pe[0]), tiled(pe[1])], axis=1).astype(BF16)

    gdim = FNO_GROUP_DIM
    dd = jnp.arange(gdim, dtype=jnp.int32)
    phase = ((dd[:, None] * dd[None, :]) % gdim).astype(F32) * (2.0 * math.pi / gdim)
    scale = 1.0 / math.sqrt(seq * gdim)
    wc = (jnp.cos(phase) * scale).astype(BF16)
    ws = (-jnp.sin(phase) * scale).astype(BF16)
    return qc, qs, pc, ps, pe, wc, ws


def _fourier(ho, heo, hee, tables, batch, seq):
    d = ho.shape[1]
    n = seq // 4
    rows = FOURIER_ROWS
    qc, qs, pc, ps, pe, wc, ws = tables
    src = lambda nrows: pl.BlockSpec((nrows, d), lambda b, kt: (b, 0))
    tab = lambda t: pl.BlockSpec((None,) + t.shape[1:], lambda b, kt: (kt, 0, 0))
    f = pl.pallas_call(
        _fourier_kernel,
        grid=(batch, n // rows),
        in_specs=[src(2 * n), src(n), src(n), tab(qc), tab(qs), tab(pc), tab(ps), tab(pe),
                  _resident(wc.shape), _resident(ws.shape)],
        out_specs=pl.BlockSpec((None, 4, rows, d), lambda b, kt: (b, 0, kt, 0)),
        out_shape=jax.ShapeDtypeStruct((batch, 4, n, d), BF16),
        scratch_shapes=[pltpu.VMEM((n, d), BF16), pltpu.VMEM((n, d), BF16),
                        pltpu.VMEM((n // 2, d), BF16), pltpu.VMEM((n // 2, d), BF16)],
        compiler_params=pltpu.CompilerParams(dimension_semantics=("arbitrary", "arbitrary"),
                                             vmem_limit_bytes=VMEM_LIMIT_BYTES),
        name="fourier",
    )(ho, heo, hee, qc, qs, pc, ps, pe, wc, ws)
    return f.reshape(batch * seq, d)


def _mid_kernel(x_ref, m_ref, p_ref, mgain_ref, wmix_ref, nffn_ref, wgu_ref, wd_ref,
                nple_ref, wpg_ref, wpp_ref, nout_ref, *refs, final, emit_h, scale_m):
    out_ref = refs[0]
    subs = _sub_tiles(x_ref.shape[0])

    def mix_in(rows):
        m = m_ref[rows, :]
        if scale_m:
            m = (m.astype(F32) * mgain_ref[...]).astype(BF16)
        return x_ref[rows, :] + _dot(m, wmix_ref[...])

    def ffn_act(x):
        h = _rmsnorm(x, nffn_ref[...]).astype(BF16)
        gu = _dot(h, wgu_ref[...])
        blk = MXU_COLS
        act = [_silu(gu[:, lo:lo + blk]) * gu[:, lo + blk:lo + 2 * blk] for lo in range(0, gu.shape[1], 2 * blk)]
        return jnp.concatenate(act, axis=1).astype(BF16)

    def ple(x, rows):
        hp = _rmsnorm(x, nple_ref[...]).astype(BF16)
        pgate = 0.5 + 0.5 * jnp.tanh(0.5 * _dot(hp, wpg_ref[...]))
        return x + pgate * _dot(p_ref[rows, :].astype(BF16), wpp_ref[...])

    xs = [mix_in(rows) for rows in subs]
    acts = [ffn_act(x) for x in xs]
    xs = [x + _dot(act, wd_ref[...]) for x, act in zip(xs, acts)]
    nrow = SUB_TILE
    xs = [ple(x, rows) for x, rows in zip(xs, subs)]
    for r0, x, rows in zip(range(0, nrow * len(subs), nrow), xs, subs):
        if final:
            out_ref[rows, :] = _rmsnorm(x, nout_ref[...])
            continue
        out_ref[rows, :] = x
        if emit_h:
            ho_ref, heo_ref, hee_ref, hn_ref = refs[1], refs[2], refs[3], refs[4]
            hn = _rmsnorm(x, nout_ref[...])
            for blk in range(hn_ref.shape[0]):
                cols = slice(blk * LANES, (blk + 1) * LANES)
                hn_ref[blk, rows, :] = hn[:, cols]
                ho_ref[r0 // 2:(r0 + nrow) // 2, cols] = (
                    hn_ref[blk, pl.ds(r0 + 1, nrow // 2, stride=2), :].astype(ho_ref.dtype))
                heo_ref[r0 // 4:(r0 + nrow) // 4, cols] = (
                    hn_ref[blk, pl.ds(r0 + 2, nrow // 4, stride=4), :].astype(heo_ref.dtype))
                hee_ref[r0 // 4:(r0 + nrow) // 4, cols] = (
                    hn_ref[blk, pl.ds(r0, nrow // 4, stride=4), :].astype(hee_ref.dtype))


def _mid(x, m, p, layer, mgain, wmix, nffn, wgu, wd, nple, wpg, wpp, nout, *, final, emit_h):
    t, d = x.shape
    tm = TOKEN_TILE
    tok = lambda width: pl.BlockSpec((tm, width), lambda i: (i, 0))
    out_specs = [tok(d)]
    out_shape = [jax.ShapeDtypeStruct((t, d), F32)]
    scratch = []
    if emit_h:
        for part in (2, 4, 4):
            out_specs.append(pl.BlockSpec((tm // part, d), lambda i: (i, 0)))
            out_shape.append(jax.ShapeDtypeStruct((t // part, d), BF16))
        scratch = [pltpu.VMEM((d // LANES, tm, LANES), F32)]
    scale_m = mgain is not None
    if mgain is None:
        mgain = (jnp.ones((1, 1, m.shape[1]), F32), 0)
    params = (mgain, wmix, nffn, wgu, wd, nple, wpg, wpp, nout)
    weights = [w for w, _ in params]
    p_spec = pl.BlockSpec((None, tm, p.shape[2]), lambda i: (layer, i, 0))
    return pl.pallas_call(
        functools.partial(_mid_kernel, final=final, emit_h=emit_h, scale_m=scale_m),
        grid=(t // tm,),
        in_specs=[tok(d), tok(m.shape[1]), p_spec] + [_layer_resident(w, idx) for w, idx in params],
        out_specs=out_specs,
        out_shape=out_shape,
        scratch_shapes=scratch,
        compiler_params=pltpu.CompilerParams(dimension_semantics=("arbitrary",),
                                             vmem_limit_bytes=VMEM_LIMIT_BYTES),
        name="mid",
    )(x, m, p, *weights)


def kernel(x, p, positions, norm_mix, ret_w_in, ret_w_out, ret_gn_gain, ret_decay_logit, fno_w_out, norm_ffn, ffn_w_gate, ffn_w_up, ffn_w_down, norm_ple, ple_w_gate, ple_w_proj, final_norm):
    batch, seq, d = x.shape
    depth = p.shape[0]
    t = batch * seq
    xt = x.reshape(t, d)
    pt = p.reshape(depth, t, p.shape[-1])
    pos = positions.reshape(t, 1)
    half = RET_DK // 2
    freq = (1.0 / (ROPE_BASE ** jnp.linspace(0.0, 1.0, half, dtype=F32))).reshape(1, half)
    fourier_tables = _fourier_tables(seq)

    bf = lambda w: w.astype(BF16)
    ret_w_in, ret_w_out, fno_w_out = bf(ret_w_in), bf(ret_w_out), bf(fno_w_out)
    blocked = lambda w: bf(w).reshape(depth, d, -1, 1, MXU_COLS)
    ffn_w_gu = jnp.concatenate([blocked(ffn_w_gate), blocked(ffn_w_up)], axis=3).reshape(depth, d, -1)
    ffn_w_down = bf(ffn_w_down)
    ple_w_gate, ple_w_proj = bf(ple_w_gate), bf(ple_w_proj)
    norm_mix, norm_ffn, norm_ple = _stacked_rows(norm_mix), _stacked_rows(norm_ffn), _stacked_rows(norm_ple)
    final_norm, ret_gn_gain = _stacked_rows(final_norm), _stacked_rows(ret_gn_gain)

    h_next = None
    for i in range(depth):
        j = i // 2
        if i % 2 == 0:
            q, k, v, g = _ret_in(xt, pos, freq, (norm_mix, i), (ret_w_in, j))
            m = _ret_core(q, k, v, g, ret_decay_logit[j].astype(F32), batch, seq)
            mgain = (ret_gn_gain, j)
            wmix = (ret_w_out, j)
        else:
            m = _fourier(*h_next, fourier_tables, batch, seq)
            mgain = None
            wmix = (fno_w_out, j)
        final = i == depth - 1
        emit_h = (not final) and (i + 1) % 2 == 1
        nout = (final_norm, 0) if final else (norm_mix, i + 1)
        outs = _mid(xt, m, pt, i, mgain, wmix, (norm_ffn, i), (ffn_w_gu, i), (ffn_w_down, i),
                    (norm_ple, i), (ple_w_gate, i), (ple_w_proj, i), nout, final=final, emit_h=emit_h)
        xt = outs[0]
        h_next = tuple(outs[1:4]) if emit_h else None
    return xt.reshape(batch, seq, d)
```

```python
import functools
import math

import jax
import jax.numpy as jnp
from jax import lax
from jax.experimental import pallas as pl
from jax.experimental.pallas import tpu as pltpu

F32 = jnp.float32
BF16 = jnp.bfloat16

NORM_EPS = 1e-6
ROPE_BASE = 10000.0
RET_HEADS = 4
RET_DK = 256
RET_DV = 512
FNO_GROUP_DIM = 256
LANES = 128
MXU_COLS = 256

VMEM_LIMIT_BYTES = 56 * 1024 * 1024

TOKEN_TILE = 512
RET_IN_TILE = 1024
SUB_TILE = 256
RET_CHUNK = 256
FOURIER_ROWS = 256
FLIP_ROWS = 256


def _resident(shape):
    nd = len(shape)
    return pl.BlockSpec(shape, lambda *_: (0,) * nd, pipeline_mode=pl.Buffered(1))


def _layer_resident(stacked, layer):
    nd = stacked.ndim
    return pl.BlockSpec((None,) + stacked.shape[1:], lambda *_: (layer,) + (0,) * (nd - 1),
                        pipeline_mode=pl.Buffered(1))


def _stacked_rows(v):
    return v.reshape(-1, 1, v.shape[-1])


def _rmsnorm(x, gain):
    ms = jnp.mean(x * x, axis=-1, keepdims=True)
    return x * lax.rsqrt(ms + NORM_EPS) * gain


def _silu(x):
    hx = 0.5 * x
    return hx + hx * jnp.tanh(hx)


def _dot(a, b):
    return jnp.dot(a, b, preferred_element_type=F32)


_TWO_OVER_PI = 0.6366197723675814
_PIO2_HI = 1.5703125
_PIO2_MID = 4.837512969970703125e-4
_PIO2_LO = 7.54978995489188216e-8


def _sincos(x):
    n = jnp.floor(x * _TWO_OVER_PI + 0.5)
    r = ((x - n * _PIO2_HI) - n * _PIO2_MID) - n * _PIO2_LO
    r2 = r * r
    s = r + r * r2 * (-1.6666654611e-1 + r2 * (8.3321608736e-3 + r2 * -1.9515295891e-4))
    c = 1.0 - 0.5 * r2 + r2 * r2 * (4.166664568298827e-2 + r2 * (-1.388731625493765e-3 + r2 * 2.443315711809948e-5))
    q = n.astype(jnp.int32)
    odd = (q & 1) == 1
    sin_x = jnp.where(odd, c, s)
    cos_x = jnp.where(odd, s, c)
    sin_x = jnp.where((q & 2) == 2, -sin_x, sin_x)
    cos_x = jnp.where(((q + 1) & 2) == 2, -cos_x, cos_x)
    return sin_x, cos_x


def _sub_tiles(rows):
    return [pl.ds(r, SUB_TILE) for r in range(0, rows, SUB_TILE)]


def _ret_in_kernel(x_ref, pos_ref, freq_ref, gain_ref, w_ref, q_ref, k_ref, v_ref, g_ref):
    qk = RET_HEADS * RET_DK
    vw = RET_HEADS * RET_DV
    half = RET_DK // 2

    subs = _sub_tiles(x_ref.shape[0])
    hs = [_rmsnorm(x_ref[rows, :], gain_ref[...]).astype(BF16) for rows in subs]

    angs = [pos_ref[rows, :].astype(F32) * freq_ref[...] for rows in subs]
    trig = [_sincos(ang) for ang in angs]

    def rope_store(proj, sin_cos, rows, out_ref, scale):
        sin, cos = sin_cos
        for hd in range(RET_HEADS):
            lo = hd * RET_DK
            x1 = proj[:, lo:lo + half]
            x2 = proj[:, lo + half:lo + RET_DK]
            out_ref[rows, lo:lo + half] = ((x1 * cos - x2 * sin) * scale).astype(out_ref.dtype)
            out_ref[rows, lo + half:lo + RET_DK] = ((x1 * sin + x2 * cos) * scale).astype(out_ref.dtype)

    for h, rows in zip(hs, subs):
        v_ref[rows, :] = _dot(h, w_ref[:, 2 * qk:2 * qk + vw]).astype(v_ref.dtype)
    for h, sin_cos, rows in zip(hs, trig, subs):
        rope_store(_dot(h, w_ref[:, 0:qk]), sin_cos, rows, q_ref, 1.0)
    for h, sin_cos, rows in zip(hs, trig, subs):
        rope_store(_dot(h, w_ref[:, qk:2 * qk]), sin_cos, rows, k_ref, RET_DK ** -0.5)
    for h, rows in zip(hs, subs):
        g_ref[rows, :] = _dot(h, w_ref[:, 2 * qk + vw:2 * qk + 2 * vw]).astype(g_ref.dtype)


def _ret_in(x, pos, freq, gain, w_in):
    t, d = x.shape
    tm = RET_IN_TILE
    qk = RET_HEADS * RET_DK
    vw = RET_HEADS * RET_DV
    tok = lambda width: pl.BlockSpec((tm, width), lambda i: (i, 0))
    (gain, gain_layer), (w_in, w_layer) = gain, w_in
    return pl.pallas_call(
        _ret_in_kernel,
        grid=(t // tm,),
        in_specs=[tok(d), tok(1), _resident(freq.shape), _layer_resident(gain, gain_layer),
                  _layer_resident(w_in, w_layer)],
        out_specs=[tok(qk), tok(qk), tok(vw), tok(vw)],
        out_shape=[jax.ShapeDtypeStruct((t, qk), BF16), jax.ShapeDtypeStruct((t, qk), BF16),
                   jax.ShapeDtypeStruct((t, vw), BF16), jax.ShapeDtypeStruct((t, vw), BF16)],
        compiler_params=pltpu.CompilerParams(dimension_semantics=("arbitrary",),
                                             vmem_limit_bytes=VMEM_LIMIT_BYTES),
        name="ret_in",
    )(x, pos, freq, gain, w_in)


def _log_sigmoid(x):
    return jnp.minimum(x, 0.0) - jnp.log(1.0 + jnp.exp(-jnp.abs(x)))


def _ret_core_kernel(logit_ref, q_ref, k_ref, v_ref, g_ref, o_ref, rf_ref, rb_ref, rfs_ref, rbs_ref):
    c_len = RET_CHUNK
    n_chunks = q_ref.shape[0] // c_len
    head = pl.program_id(1)

    def log_gamma(direction, shape):
        return _log_sigmoid(jnp.full(shape, logit_ref[direction, head], F32))

    row = lax.broadcasted_iota(jnp.int32, (c_len, RET_DK), 0).astype(F32)
    lgf = log_gamma(0, (c_len, RET_DK))
    lgb = log_gamma(1, (c_len, RET_DK))
    xi_f = jnp.exp(lgf * (row + 1.0)).astype(BF16)
    zeta_f = jnp.exp(lgf * (c_len - 1.0 - row)).astype(BF16)
    xi_b = jnp.exp(lgb * (c_len - row)).astype(BF16)
    zeta_b = jnp.exp(lgb * row).astype(BF16)
    gchunk_f = jnp.exp(log_gamma(0, (1, 1)) * float(c_len))
    gchunk_b = jnp.exp(log_gamma(1, (1, 1)) * float(c_len))

    ii = lax.broadcasted_iota(jnp.int32, (c_len, c_len), 0)
    jj = lax.broadcasted_iota(jnp.int32, (c_len, c_len), 1)
    dist = (ii - jj).astype(F32)
    decay = jnp.where(ii >= jj,
                      jnp.exp(log_gamma(0, (c_len, c_len)) * jnp.maximum(dist, 0.0)),
                      jnp.exp(log_gamma(1, (c_len, c_len)) * jnp.maximum(-dist, 0.0)))

    def chunk(ref, c):
        return ref[pl.ds(pl.multiple_of(c * c_len, c_len), c_len), :]

    def kt_v(c, zeta):
        return lax.dot_general(chunk(k_ref, c) * zeta, chunk(v_ref, c), (((0,), (0,)), ((), ())),
                               preferred_element_type=F32)

    rf_ref[...] = jnp.zeros_like(rf_ref)
    rb_ref[...] = jnp.zeros_like(rb_ref)
    rfs_ref[0] = jnp.zeros(rfs_ref.shape[1:], rfs_ref.dtype)
    rbs_ref[n_chunks - 1] = jnp.zeros(rbs_ref.shape[1:], rbs_ref.dtype)

    def state_step(i, carry):
        cb = n_chunks - 1 - i
        rf = gchunk_f * rf_ref[...] + kt_v(i, zeta_f)
        rf_ref[...] = rf
        rfs_ref[i + 1] = rf.astype(rfs_ref.dtype)
        rb = gchunk_b * rb_ref[...] + kt_v(cb, zeta_b)
        rb_ref[...] = rb
        rbs_ref[cb - 1] = rb.astype(rbs_ref.dtype)
        return carry

    lax.fori_loop(0, n_chunks - 1, state_step, 0, unroll=True)

    def retention(c):
        q_c = chunk(q_ref, c)
        scores = lax.dot_general(q_c, chunk(k_ref, c), (((1,), (1,)), ((), ())), preferred_element_type=F32)
        return (_dot((scores * decay).astype(BF16), chunk(v_ref, c))
                + _dot(q_c * xi_f, rfs_ref[c]) + _dot(q_c * xi_b, rbs_ref[c]))

    def norm_gate_store(c, o):
        mu = jnp.mean(o, axis=-1, keepdims=True)
        cen = o - mu
        var = jnp.mean(cen * cen, axis=-1, keepdims=True)
        normed = (cen * lax.rsqrt(var + NORM_EPS)).astype(BF16)
        o_ref[pl.ds(pl.multiple_of(c * c_len, c_len), c_len), :] = _silu(chunk(g_ref, c)) * normed

    def out_step(c, carry):
        norm_gate_store(c, retention(c))
        return carry

    lax.fori_loop(0, n_chunks, out_step, 0, unroll=True)


def _ret_core(q, k, v, g, decay_logit, batch, seq):
    n_chunks = seq // RET_CHUNK
    blk = lambda width: pl.BlockSpec((seq, width), lambda b, h: (b, h))
    return pl.pallas_call(
        _ret_core_kernel,
        grid=(batch, RET_HEADS),
        in_specs=[pl.BlockSpec(memory_space=pltpu.SMEM), blk(RET_DK), blk(RET_DK), blk(RET_DV), blk(RET_DV)],
        out_specs=blk(RET_DV),
        out_shape=jax.ShapeDtypeStruct(v.shape, BF16),
        scratch_shapes=[pltpu.VMEM((RET_DK, RET_DV), F32), pltpu.VMEM((RET_DK, RET_DV), F32),
                        pltpu.VMEM((n_chunks, RET_DK, RET_DV), BF16),
                        pltpu.VMEM((n_chunks, RET_DK, RET_DV), BF16)],
        compiler_params=pltpu.CompilerParams(dimension_semantics=("arbitrary", "arbitrary"),
                                             vmem_limit_bytes=VMEM_LIMIT_BYTES),
        name="ret_core",
    )(decay_logit, q, k, v, g)


def _fold_rows(src_ref, sum_ref, dif_ref):
    blk = FLIP_ROWS
    n = src_ref.shape[0]
    ii = lax.broadcasted_iota(jnp.int32, (blk, blk), 0)
    jj = lax.broadcasted_iota(jnp.int32, (blk, blk), 1)
    flip = jnp.where(ii + jj == blk - 1, 1.0, 0.0).astype(BF16)
    for i in range(n // 2 // blk):
        lo = src_ref[i * blk:(i + 1) * blk, :].astype(F32)
        hi = _dot(flip, src_ref[n - (i + 1) * blk:n - i * blk, :])
        sum_ref[i * blk:(i + 1) * blk, :] = (lo + hi).astype(sum_ref.dtype)
        dif_ref[i * blk:(i + 1) * blk, :] = (lo - hi).astype(dif_ref.dtype)


def _fourier_kernel(ho_ref, heo_ref, hee_ref, qc_ref, qs_ref, pc_ref, ps_ref, pe_ref, wc_ref, ws_ref,
                    f_ref, qsum_ref, qdif_ref, psum_ref, pdif_ref):
    rows = f_ref.shape[1]
    d = ho_ref.shape[1]

    @pl.when(pl.program_id(1) == 0)
    def _():
        _fold_rows(ho_ref, qsum_ref, qdif_ref)
        _fold_rows(heo_ref, psum_ref, pdif_ref)

    def channel_dft(a, b):
        a = a.astype(BF16)
        b = b.astype(BF16)
        return [_dot(a[:, lo:lo + FNO_GROUP_DIM], wc_ref[...]) + _dot(b[:, lo:lo + FNO_GROUP_DIM], ws_ref[...])
                for lo in range(0, d, FNO_GROUP_DIM)]

    qa = _dot(qc_ref[...], qsum_ref[...])
    qb = _dot(qs_ref[...], qdif_ref[...])
    pe = _dot(pe_ref[...], hee_ref[...])
    fq0 = channel_dft(qa[0:rows], qb[0:rows])
    fq1 = channel_dft(qa[rows:2 * rows], qb[rows:2 * rows])
    fpe = channel_dft(pe[0:rows], pe[rows:2 * rows])
    fpo = channel_dft(_dot(pc_ref[...], psum_ref[...]), _dot(ps_ref[...], pdif_ref[...]))
    for grp in range(d // FNO_GROUP_DIM):
        cols = slice(grp * FNO_GROUP_DIM, (grp + 1) * FNO_GROUP_DIM)
        even = fpe[grp] + fpo[grp]
        odd = fpe[grp] - fpo[grp]
        f_ref[0, :, cols] = (even + fq0[grp]).astype(f_ref.dtype)
        f_ref[1, :, cols] = (odd + fq1[grp]).astype(f_ref.dtype)
        f_ref[2, :, cols] = (even - fq0[grp]).astype(f_ref.dtype)
        f_ref[3, :, cols] = (odd - fq1[grp]).astype(f_ref.dtype)


def _fourier_tables(seq):
    n = seq // 4
    rows = FOURIER_ROWS

    def cis(phase_index):
        phase = (phase_index % seq).astype(F32) * (2.0 * math.pi / seq)
        return jnp.cos(phase), jnp.sin(phase)

    def cis_table(alpha, n_rows, b):
        step = 32
        kh = jnp.arange(n_rows // step, dtype=jnp.int32)[:, None]
        kl = jnp.arange(step, dtype=jnp.int32)[:, None]
        ch, sh = cis(alpha * step * kh * b[None, :])
        cl, sl = cis(alpha * kl * b[None, :])
        ch, sh, cl, sl = ch[:, None, :], sh[:, None, :], cl[None], sl[None]
        return (ch * cl - sh * sl).reshape(n_rows, -1), (sh * cl + ch * sl).reshape(n_rows, -1)

    def tiled(x):
        return x.reshape(-1, rows, x.shape[-1])

    j = jnp.arange(n, dtype=jnp.int32)
    q_cos, q_sin = cis_table(1, 2 * n, 2 * j + 1)
    qc = jnp.concatenate([tiled(q_cos[:n]), tiled(q_cos[n:])], axis=1).astype(BF16)
    qs = jnp.concatenate([tiled(q_sin[:n]), tiled(q_sin[n:])], axis=1).astype(BF16)
    po = cis_table(2, n, 2 * j[:n // 2] + 1)
    pc, ps = tiled(po[0]).astype(BF16), tiled(po[1]).astype(BF16)
    pe = cis_table(4, n, j)
    pe = jnp.concatenate([tiled(pe[0]), tiled(pe[1])], axis=1).astype(BF16)

    gdim = FNO_GROUP_DIM
    dd = jnp.arange(gdim, dtype=jnp.int32)
    phase = ((dd[:, None] * dd[None, :]) % gdim).astype(F32) * (2.0 * math.pi / gdim)
    scale = 1.0 / math.sqrt(seq * gdim)
    wc = (jnp.cos(phase) * scale).astype(BF16)
    ws = (-jnp.sin(phase) * scale).astype(BF16)
    return qc, qs, pc, ps, pe, wc, ws


def _fourier(ho, heo, hee, tables, batch, seq):
    d = ho.shape[1]
    n = seq // 4
    rows = FOURIER_ROWS
    qc, qs, pc, ps, pe, wc, ws = tables
    src = lambda nrows: pl.BlockSpec((nrows, d), lambda b, kt: (b, 0))
    tab = lambda t: pl.BlockSpec((None,) + t.shape[1:], lambda b, kt: (kt, 0, 0))
    f = pl.pallas_call(
        _fourier_kernel,
        grid=(batch, n // rows),
        in_specs=[src(2 * n), src(n), src(n), tab(qc), tab(qs), tab(pc), tab(ps), tab(pe),
                  _resident(wc.shape), _resident(ws.shape)],
        out_specs=pl.BlockSpec((None, 4, rows, d), lambda b, kt: (b, 0, kt, 0)),
        out_shape=jax.ShapeDtypeStruct((batch, 4, n, d), BF16),
        scratch_shapes=[pltpu.VMEM((n, d), BF16), pltpu.VMEM((n, d), BF16),
                        pltpu.VMEM((n // 2, d), BF16), pltpu.VMEM((n // 2, d), BF16)],
        compiler_params=pltpu.CompilerParams(dimension_semantics=("arbitrary", "arbitrary"),
                                             vmem_limit_bytes=VMEM_LIMIT_BYTES),
        name="fourier",
    )(ho, heo, hee, qc, qs, pc, ps, pe, wc, ws)
    return f.reshape(batch * seq, d)


def _mid_kernel(x_ref, m_ref, p_ref, wmix_ref, nffn_ref, wg_ref, wu_ref, wd_ref,
                nple_ref, wpg_ref, wpp_ref, nout_ref, *refs, final, emit_h):
    out_ref = refs[0]
    subs = _sub_tiles(x_ref.shape[0])

    def mix_in(rows):
        return x_ref[rows, :] + _dot(m_ref[rows, :], wmix_ref[...])

    def ffn_act(x):
        h = _rmsnorm(x, nffn_ref[...]).astype(BF16)
        act = [_silu(_dot(h, wg_ref[:, lo:lo + MXU_COLS])) * _dot(h, wu_ref[:, lo:lo + MXU_COLS])
               for lo in range(0, wg_ref.shape[1], MXU_COLS)]
        return jnp.concatenate(act, axis=1).astype(BF16)

    def ple(x, rows):
        hp = _rmsnorm(x, nple_ref[...]).astype(BF16)
        pb = p_ref[rows, :].astype(BF16)
        out = []
        for lo in range(0, x.shape[1], MXU_COLS):
            cols = slice(lo, lo + MXU_COLS)
            pgate = 0.5 + 0.5 * jnp.tanh(0.5 * _dot(hp, wpg_ref[:, cols]))
            out.append(x[:, cols] + pgate * _dot(pb, wpp_ref[:, cols]))
        return jnp.concatenate(out, axis=1)

    xs = [mix_in(rows) for rows in subs]
    acts = [ffn_act(x) for x in xs]
    xs = [x + _dot(act, wd_ref[...]) for x, act in zip(xs, acts)]
    nrow = SUB_TILE
    xs = [ple(x, rows) for x, rows in zip(xs, subs)]
    for r0, x, rows in zip(range(0, nrow * len(subs), nrow), xs, subs):
        if final:
            out_ref[rows, :] = _rmsnorm(x, nout_ref[...])
            continue
        out_ref[rows, :] = x
        if emit_h:
            ho_ref, heo_ref, hee_ref, hn_ref = refs[1], refs[2], refs[3], refs[4]
            hn = _rmsnorm(x, nout_ref[...])
            for blk in range(hn_ref.shape[0]):
                cols = slice(blk * LANES, (blk + 1) * LANES)
                hn_ref[blk, rows, :] = hn[:, cols]
                ho_ref[r0 // 2:(r0 + nrow) // 2, cols] = (
                    hn_ref[blk, pl.ds(r0 + 1, nrow // 2, stride=2), :].astype(ho_ref.dtype))
                heo_ref[r0 // 4:(r0 + nrow) // 4, cols] = (
                    hn_ref[blk, pl.ds(r0 + 2, nrow // 4, stride=4), :].astype(heo_ref.dtype))
                hee_ref[r0 // 4:(r0 + nrow) // 4, cols] = (
                    hn_ref[blk, pl.ds(r0, nrow // 4, stride=4), :].astype(hee_ref.dtype))


def _mid(x, m, p, layer, wmix, nffn, wg, wu, wd, nple, wpg, wpp, nout, *, final, emit_h):
    t, d = x.shape
    tm = TOKEN_TILE
    tok = lambda width: pl.BlockSpec((tm, width), lambda i: (i, 0))
    out_specs = [tok(d)]
    out_shape = [jax.ShapeDtypeStruct((t, d), F32)]
    scratch = []
    if emit_h:
        for part in (2, 4, 4):
            out_specs.append(pl.BlockSpec((tm // part, d), lambda i: (i, 0)))
            out_shape.append(jax.ShapeDtypeStruct((t // part, d), BF16))
        scratch = [pltpu.VMEM((d // LANES, tm, LANES), F32)]
    params = (wmix, nffn, wg, wu, wd, nple, wpg, wpp, nout)
    weights = [w for w, _ in params]
    p_spec = pl.BlockSpec((None, tm, p.shape[2]), lambda i: (layer, i, 0))
    return pl.pallas_call(
        functools.partial(_mid_kernel, final=final, emit_h=emit_h),
        grid=(t // tm,),
        in_specs=[tok(d), tok(m.shape[1]), p_spec] + [_layer_resident(w, idx) for w, idx in params],
        out_specs=out_specs,
        out_shape=out_shape,
        scratch_shapes=scratch,
        compiler_params=pltpu.CompilerParams(dimension_semantics=("arbitrary",),
                                             vmem_limit_bytes=VMEM_LIMIT_BYTES),
        name="mid",
    )(x, m, p, *weights)


def kernel(x, p, positions, norm_mix, ret_w_in, ret_w_out, ret_gn_gain, ret_decay_logit, fno_w_out, norm_ffn, ffn_w_gate, ffn_w_up, ffn_w_down, norm_ple, ple_w_gate, ple_w_proj, final_norm):
    batch, seq, d = x.shape
    depth = p.shape[0]
    t = batch * seq
    xt = x.reshape(t, d)
    pt = p.reshape(depth, t, p.shape[-1])
    pos = positions.reshape(t, 1)
    half = RET_DK // 2
    freq = (1.0 / (ROPE_BASE ** jnp.linspace(0.0, 1.0, half, dtype=F32))).reshape(1, half)
    fourier_tables = _fourier_tables(seq)

    bf = lambda w: w.astype(BF16)
    ret_w_out = ret_w_out * ret_gn_gain[:, :, None]
    ret_w_in, ret_w_out, fno_w_out = bf(ret_w_in), bf(ret_w_out), bf(fno_w_out)
    ffn_w_gate, ffn_w_up, ffn_w_down = bf(ffn_w_gate), bf(ffn_w_up), bf(ffn_w_down)
    ple_w_gate, ple_w_proj = bf(ple_w_gate), bf(ple_w_proj)
    norm_mix, norm_ffn, norm_ple = _stacked_rows(norm_mix), _stacked_rows(norm_ffn), _stacked_rows(norm_ple)
    final_norm = _stacked_rows(final_norm)

    h_next = None
    for i in range(depth):
        j = i // 2
        if i % 2 == 0:
            q, k, v, g = _ret_in(xt, pos, freq, (norm_mix, i), (ret_w_in, j))
            m = _ret_core(q, k, v, g, ret_decay_logit[j].astype(F32), batch, seq)
            wmix = (ret_w_out, j)
        else:
            m = _fourier(*h_next, fourier_tables, batch, seq)
            wmix = (fno_w_out, j)
        final = i == depth - 1
        emit_h = (not final) and (i + 1) % 2 == 1
        nout = (final_norm, 0) if final else (norm_mix, i + 1)
        outs = _mid(xt, m, pt, i, wmix, (norm_ffn, i), (ffn_w_gate, i), (ffn_w_up, i), (ffn_w_down, i),
                    (norm_ple, i), (ple_w_gate, i), (ple_w_proj, i), nout, final=final, emit_h=emit_h)
        xt = outs[0]
        h_next = tuple(outs[1:4]) if emit_h else None
    return xt.reshape(batch, seq, d)
```
